```python
import math
import jax, jax.numpy as jnp
from jax import lax
import numpy as np

D_MODEL = 1024
BATCH = 32
SEQ = 2048
DEPTH = 1
DEC_BATCH = 32
DEC_SEQ = 32
PAST_LEN = 4096

CHUNK = 64
WINDOW = 128
WINDOW_CHUNKS = WINDOW // CHUNK
HEAD_DIM = 64
N_HEADS = D_MODEL // HEAD_DIM
N_KV_HEADS = max(1, N_HEADS // 8)
GROUP = N_HEADS // N_KV_HEADS
ATTN_WIDTH = N_HEADS * HEAD_DIM
KV_WIDTH = N_KV_HEADS * HEAD_DIM
CONV_CHANNELS = D_MODEL
CONV_WIDTH = 31
N_BRANCHES = 2
D_FF = ((8 * D_MODEL + 3 * 256 - 1) // (3 * 256)) * 256
N_BUCKETS = 32
MAX_DISTANCE = 128
EPS = 1e-6
NEG_INF = -1e30
IN_WIDTH = ATTN_WIDTH + 2 * KV_WIDTH + 2 * CONV_CHANNELS + N_BRANCHES * D_MODEL

kernel_name = "hybrid_swa_sink_conformer_conv_stream_step"


def rmsnorm(x, g):
    xf = x.astype(jnp.float32)
    y = xf * lax.rsqrt(jnp.mean(xf * xf, axis=-1, keepdims=True) + EPS)
    return (y * g.astype(jnp.float32)).astype(x.dtype)


def layernorm(x, g, b):
    xf = x.astype(jnp.float32)
    mu = jnp.mean(xf, axis=-1, keepdims=True)
    xc = xf - mu
    y = xc * lax.rsqrt(jnp.mean(xc * xc, axis=-1, keepdims=True) + EPS)
    return (y * g.astype(jnp.float32) + b.astype(jnp.float32)).astype(x.dtype)


def t5_bucket(rel):
    half = N_BUCKETS // 2
    max_exact = half // 2
    ret = jnp.where(rel > 0, half, 0)
    n = jnp.abs(rel)
    nf = jnp.maximum(n, 1).astype(jnp.float32)
    large = max_exact + (jnp.log(nf / max_exact) / math.log(MAX_DISTANCE / max_exact)
                         * (half - max_exact)).astype(jnp.int32)
    large = jnp.minimum(large, half - 1)
    return ret + jnp.where(n < max_exact, n, large)


def rel_bias(table, rel):
    b = table[t5_bucket(rel)].astype(jnp.float32)
    return jnp.transpose(b, (2, 0, 1)).reshape(N_KV_HEADS, GROUP, rel.shape[0], rel.shape[1])


def sink_attention(q, k, v, bias, valid, sink):
    scale = HEAD_DIM ** -0.5
    s = jnp.einsum('bcqkgd,bcjkd->bckgqj', q.astype(jnp.float32), k.astype(jnp.float32)) * scale + bias
    s = jnp.where(valid[None, :, None, None, None, :], s, NEG_INF)
    sk = sink.astype(jnp.float32).reshape(1, 1, N_KV_HEADS, GROUP, 1, 1)
    m = jnp.maximum(jnp.max(s, axis=-1, keepdims=True), sk)
    p = jnp.exp(s - m)
    denom = jnp.sum(p, axis=-1, keepdims=True) + jnp.exp(sk - m)
    o = jnp.einsum('bckgqj,bcjkd->bcqkgd', p / denom, v.astype(jnp.float32))
    return o.astype(q.dtype)


def split_projection(h, w_in):
    B, T = h.shape[0], h.shape[1]
    z = h @ w_in
    idx = [ATTN_WIDTH, ATTN_WIDTH + KV_WIDTH, ATTN_WIDTH + 2 * KV_WIDTH,
           ATTN_WIDTH + 2 * KV_WIDTH + CONV_CHANNELS, ATTN_WIDTH + 2 * KV_WIDTH + 2 * CONV_CHANNELS]
    q, k, v, u_a, u_b, gl = jnp.split(z, idx, axis=-1)
    q = q.reshape(B, T, N_KV_HEADS, GROUP, HEAD_DIM)
    k = k.reshape(B, T, N_KV_HEADS, HEAD_DIM)
    v = v.reshape(B, T, N_KV_HEADS, HEAD_DIM)
    u = u_a * jax.nn.sigmoid(u_b)
    g = jax.nn.sigmoid(gl).reshape(B, T, N_BRANCHES, D_MODEL)
    return q, k, v, u, g


def prompt_attention(q, k, v, table, sink):
    B, T = q.shape[0], q.shape[1]
    n_c = T // CHUNK
    pad = WINDOW_CHUNKS * CHUNK
    qb = q.reshape(B, n_c, CHUNK, N_KV_HEADS, GROUP, HEAD_DIM)

    def band(t):
        tp = jnp.pad(t, ((0, 0), (pad, 0), (0, 0), (0, 0)))
        tp = tp.reshape(B, n_c + WINDOW_CHUNKS, CHUNK, N_KV_HEADS, HEAD_DIM)
        return jnp.concatenate([tp[:, i:i + n_c] for i in range(WINDOW_CHUNKS + 1)], axis=2)

    J = (WINDOW_CHUNKS + 1) * CHUNK
    rel = jnp.arange(J)[None, :] - pad - jnp.arange(CHUNK)[:, None]
    key_pos = jnp.arange(n_c)[:, None] * CHUNK - pad + jnp.arange(J)[None, :]
    valid = key_pos >= 0
    o = sink_attention(qb, band(k), band(v), rel_bias(table, rel), valid, sink)
    return o.reshape(B, T, ATTN_WIDTH)


def sample_attention(q, k_all, v_all, table, sink):
    B, S = q.shape[0], q.shape[1]
    J = k_all.shape[1]
    rel = jnp.arange(J)[None, :] - WINDOW - jnp.arange(S)[:, None]
    valid = jnp.ones((1, J), dtype=bool)
    o = sink_attention(q[:, None], k_all[:, None], v_all[:, None], rel_bias(table, rel), valid, sink)
    return o.reshape(B, S, ATTN_WIDTH)


def conv_branch(u_hist, w_dw, b_dw, ln_g, ln_b, w_pw):
    rhs = w_dw.reshape(CONV_WIDTH, 1, CONV_CHANNELS).astype(u_hist.dtype)
    y = lax.conv_general_dilated(u_hist, rhs, window_strides=(1,), padding='VALID',
                                 dimension_numbers=('NWC', 'WIO', 'NWC'),
                                 feature_group_count=CONV_CHANNELS) + b_dw
    y = jax.nn.silu(layernorm(y, ln_g, ln_b))
    return y @ w_pw


def merge_and_ffn(x, attn, conv, g, w_out, g_post_mix, g_pre_ffn, w_ffn_gate, w_ffn_up, w_ffn_down, g_post_ffn):
    m = g[:, :, 0] * attn + g[:, :, 1] * conv
    x = x + rmsnorm(m @ w_out, g_post_mix)
    h = rmsnorm(x, g_pre_ffn)
    f = (jax.nn.silu(h @ w_ffn_gate) * (h @ w_ffn_up)) @ w_ffn_down
    return x + rmsnorm(f, g_post_ffn)


def setup_inputs(seed: int = 0) -> dict:
    key = jax.random.key(seed)
    ks = jax.random.split(key, 22)
    nrm = jax.random.normal
    f32 = jnp.float32
    return {
        "x_prompt": nrm(ks[0], (BATCH, SEQ, D_MODEL), f32),
        "x_sample": nrm(ks[1], (DEC_BATCH, DEC_SEQ, D_MODEL), f32),
        "cache_k": nrm(ks[2], (DEPTH, DEC_BATCH, WINDOW, N_KV_HEADS, HEAD_DIM), f32),
        "cache_v": nrm(ks[3], (DEPTH, DEC_BATCH, WINDOW, N_KV_HEADS, HEAD_DIM), f32),
        "state_conv": 0.5 * nrm(ks[4], (DEPTH, DEC_BATCH, CONV_WIDTH - 1, CONV_CHANNELS), f32),
        "rel_bias_table": 0.5 * nrm(ks[5], (N_BUCKETS, N_HEADS), f32),
        "attn_sink": 0.5 * nrm(ks[6], (DEPTH, N_HEADS), f32),
        "g_pre_mix": 1.0 + 0.1 * nrm(ks[7], (DEPTH, D_MODEL), f32),
        "w_in": nrm(ks[8], (DEPTH, D_MODEL, IN_WIDTH), f32) * D_MODEL ** -0.5,
        "w_dw": nrm(ks[9], (DEPTH, CONV_WIDTH, CONV_CHANNELS), f32) * CONV_WIDTH ** -0.5,
        "b_dw": 0.02 * nrm(ks[10], (DEPTH, CONV_CHANNELS), f32),
        "ln_g": 1.0 + 0.1 * nrm(ks[11], (DEPTH, CONV_CHANNELS), f32),
        "ln_b": 0.02 * nrm(ks[12], (DEPTH, CONV_CHANNELS), f32),
        "w_conv_out": nrm(ks[13], (DEPTH, CONV_CHANNELS, D_MODEL), f32) * CONV_CHANNELS ** -0.5,
        "w_out": nrm(ks[14], (DEPTH, D_MODEL, D_MODEL), f32) * D_MODEL ** -0.5,
        "g_post_mix": 1.0 + 0.1 * nrm(ks[15], (DEPTH, D_MODEL), f32),
        "g_pre_ffn": 1.0 + 0.1 * nrm(ks[16], (DEPTH, D_MODEL), f32),
        "w_ffn_gate": nrm(ks[17], (DEPTH, D_MODEL, D_FF), f32) * D_MODEL ** -0.5,
        "w_ffn_up": nrm(ks[18], (DEPTH, D_MODEL, D_FF), f32) * D_MODEL ** -0.5,
        "w_ffn_down": nrm(ks[19], (DEPTH, D_FF, D_MODEL), f32) * D_FF ** -0.5,
        "g_post_ffn": 1.0 + 0.1 * nrm(ks[20], (DEPTH, D_MODEL), f32),
    }


def reference(x_prompt, x_sample, cache_k, cache_v, state_conv, rel_bias_table, attn_sink,
              g_pre_mix, w_in, w_dw, b_dw, ln_g, ln_b, w_conv_out, w_out, g_post_mix,
              g_pre_ffn, w_ffn_gate, w_ffn_up, w_ffn_down, g_post_ffn):
    yp = x_prompt
    ys = x_sample
    kp_l, vp_l, cp_l, ks_l, vs_l, cs_l = [], [], [], [], [], []
    for l in range(DEPTH):
        h = rmsnorm(yp, g_pre_mix[l])
        q, k, v, u, g = split_projection(h, w_in[l])
        attn = prompt_attention(q, k, v, rel_bias_table, attn_sink[l])
        u_hist = jnp.pad(u, ((0, 0), (CONV_WIDTH - 1, 0), (0, 0)))
        conv = conv_branch(u_hist, w_dw[l], b_dw[l], ln_g[l], ln_b[l], w_conv_out[l])
        kp_l.append(k[:, -WINDOW:])
        vp_l.append(v[:, -WINDOW:])
        cp_l.append(u[:, -(CONV_WIDTH - 1):])
        yp = merge_and_ffn(yp, attn, conv, g, w_out[l], g_post_mix[l], g_pre_ffn[l],
                           w_ffn_gate[l], w_ffn_up[l], w_ffn_down[l], g_post_ffn[l])
        h = rmsnorm(ys, g_pre_mix[l])
        q, k, v, u, g = split_projection(h, w_in[l])
        k_all = jnp.concatenate([cache_k[l], k], axis=1)
        v_all = jnp.concatenate([cache_v[l], v], axis=1)
        attn = sample_attention(q, k_all, v_all, rel_bias_table, attn_sink[l])
        u_hist = jnp.concatenate([state_conv[l], u], axis=1)
        conv = conv_branch(u_hist, w_dw[l], b_dw[l], ln_g[l], ln_b[l], w_conv_out[l])
        ks_l.append(k_all[:, -WINDOW:])
        vs_l.append(v_all[:, -WINDOW:])
        cs_l.append(u_hist[:, -(CONV_WIDTH - 1):])
        ys = merge_and_ffn(ys, attn, conv, g, w_out[l], g_post_mix[l], g_pre_ffn[l],
                           w_ffn_gate[l], w_ffn_up[l], w_ffn_down[l], g_post_ffn[l])
    return (yp, ys, jnp.stack(kp_l), jnp.stack(vp_l), jnp.stack(cp_l),
            jnp.stack(ks_l), jnp.stack(vs_l), jnp.stack(cs_l))
```

```python
import functools
import math

import jax
import jax.numpy as jnp
from jax import lax
from jax.experimental import pallas as pl
from jax.experimental.pallas import tpu as pltpu

D_MODEL = 1024
HEAD_DIM = 64
N_HEADS = 16
N_KV_HEADS = 2
GROUP = N_HEADS // N_KV_HEADS
KV_WIDTH = N_KV_HEADS * HEAD_DIM
WINDOW = 128
CHUNK = 64
CONV_WIDTH = 31
CONV_STATE = CONV_WIDTH - 1
CONV_HIST = 32
CONV_ROWS = 32
D_FF = 2816
N_BUCKETS = 32
MAX_DISTANCE = 128
EPS = 1e-6
NEG_INF = -1e30
IN_WIDTH = D_MODEL + 2 * KV_WIDTH + 2 * D_MODEL + 2 * D_MODEL

_Q0 = 0
_K0 = D_MODEL
_V0 = _K0 + KV_WIDTH
_UA0 = _V0 + KV_WIDTH
_UB0 = _UA0 + D_MODEL
_GA0 = _UB0 + D_MODEL
_GB0 = _GA0 + D_MODEL

VMEM_LIMIT_BYTES = 56 * 1024 * 1024


def _t5_bucket(rel):
    half = N_BUCKETS // 2
    max_exact = half // 2
    ret = jnp.where(rel > 0, half, 0)
    n = jnp.abs(rel)
    nf = jnp.maximum(n, 1).astype(jnp.float32)
    large = max_exact + (jnp.log(nf / max_exact) / math.log(MAX_DISTANCE / max_exact)
                         * (half - max_exact)).astype(jnp.int32)
    large = jnp.minimum(large, half - 1)
    return ret + jnp.where(n < max_exact, n, large)


def _rms(x, g):
    return x * lax.rsqrt(jnp.mean(x * x, axis=-1, keepdims=True) + EPS) * g


def _bf16_dot(a, w):
    return jnp.dot(a.astype(jnp.bfloat16), w, preferred_element_type=jnp.float32)


def _mixer_kernel(*refs, nb, tb, ch, carry_from_inputs):
    it = iter(refs)
    table_ref = next(it)
    sink_ref = next(it)
    bucket_ref = next(it)
    x_ref = next(it)
    if carry_from_inputs:
        ck_ref = next(it)
        cv_ref = next(it)
        cs_ref = next(it)
    g_pre_ref = next(it)
    w_in_ref = next(it)
    w_dw_ref = next(it)
    b_dw_ref = next(it)
    ln_g_ref = next(it)
    ln_b_ref = next(it)
    w_co_ref = next(it)
    w_out_ref = next(it)
    g_post_ref = next(it)
    x1_ref = next(it)
    kwin_ref = next(it)
    vwin_ref = next(it)
    cst_ref = next(it)
    bias_scr = next(it)
    k_scr = next(it)
    v_scr = next(it)
    u_scr = next(it)
    attn_scr = next(it)
    conv_scr = next(it)
    sh_scr = next(it)

    nkeys = WINDOW + ch
    n_chunks = tb // ch
    first_step = (pl.program_id(0) == 0) & (pl.program_id(1) == 0)
    seq_tile = pl.program_id(1)

    @pl.when(first_step)
    def _build_bias():
        bucket = bucket_ref[...]
        bias_scr[...] = jnp.zeros_like(bias_scr)

        def body(b, carry):
            hit = bucket == b
            for h in range(N_HEADS):
                bias_scr[h] = jnp.where(hit, table_ref[b, h], bias_scr[h])
            return carry

        lax.fori_loop(0, N_BUCKETS, body, 0)

    if carry_from_inputs:
        for b in range(nb):
            k_scr[b, 0:WINDOW, :] = ck_ref[b].astype(jnp.bfloat16)
            v_scr[b, 0:WINDOW, :] = cv_ref[b].astype(jnp.bfloat16)
            u_scr[b, 0:CONV_HIST - CONV_STATE, :] = jnp.zeros((CONV_HIST - CONV_STATE, D_MODEL), jnp.float32)
            u_scr[b, CONV_HIST - CONV_STATE:CONV_HIST, :] = cs_ref[b]
    else:
        @pl.when(seq_tile == 0)
        def _reset_history():
            for b in range(nb):
                k_scr[b, 0:WINDOW, :] = jnp.zeros((WINDOW, KV_WIDTH), jnp.bfloat16)
                v_scr[b, 0:WINDOW, :] = jnp.zeros((WINDOW, KV_WIDTH), jnp.bfloat16)
                u_scr[b, 0:CONV_HIST, :] = jnp.zeros((CONV_HIST, D_MODEL), jnp.float32)

    x = x_ref[...].reshape(nb * tb, D_MODEL)
    h = _rms(x, g_pre_ref[...]).astype(jnp.bfloat16)

    qkv = jnp.dot(h, w_in_ref[:, _Q0:_UA0], preferred_element_type=jnp.float32)
    q = qkv[:, _Q0:_K0] * (HEAD_DIM ** -0.5)
    k = qkv[:, _K0:_V0]
    v = qkv[:, _V0:_UA0]
    for b in range(nb):
        rows = slice(b * tb, (b + 1) * tb)
        k_scr[b, WINDOW:WINDOW + tb, :] = k[rows].astype(jnp.bfloat16)
        v_scr[b, WINDOW:WINDOW + tb, :] = v[rows].astype(jnp.bfloat16)
        if tb >= WINDOW:
            kwin_ref[b] = k[b * tb + tb - WINDOW:(b + 1) * tb]
            vwin_ref[b] = v[b * tb + tb - WINDOW:(b + 1) * tb]
        else:
            kwin_ref[b, 0:WINDOW - tb, :] = ck_ref[b, tb:WINDOW, :]
            vwin_ref[b, 0:WINDOW - tb, :] = cv_ref[b, tb:WINDOW, :]
            kwin_ref[b, WINDOW - tb:WINDOW, :] = k[rows]
            vwin_ref[b, WINDOW - tb:WINDOW, :] = v[rows]

    qb = q.astype(jnp.bfloat16)
    key_iota = lax.broadcasted_iota(jnp.int32, (ch, nkeys), 1)
    for b in range(nb):
        for c in range(n_chunks):
            r0 = b * tb + c * ch
            keys = k_scr[b, c * ch:c * ch + nkeys, :]
            vals = v_scr[b, c * ch:c * ch + nkeys, :]
            need_mask = (not carry_from_inputs) and (c * ch < WINDOW)
            if need_mask:
                first_pos = seq_tile * tb + (c * ch - WINDOW)
                valid = (key_iota + first_pos) >= 0
            for kh in range(N_KV_HEADS):
                kk = keys[:, kh * HEAD_DIM:(kh + 1) * HEAD_DIM]
                vv = vals[:, kh * HEAD_DIM:(kh + 1) * HEAD_DIM]
                qs = jnp.concatenate(
                    [qb[r0:r0 + ch, (kh * GROUP + g) * HEAD_DIM:(kh * GROUP + g + 1) * HEAD_DIM]
                     for g in range(GROUP)], axis=0)
                s = lax.dot_general(qs, kk, (((1,), (1,)), ((), ())),
                                    preferred_element_type=jnp.float32)
                ps = []
                rden = []
                for g in range(GROUP):
                    hh = kh * GROUP + g
                    sg = s[g * ch:(g + 1) * ch] + bias_scr[hh]
                    if need_mask:
                        sg = jnp.where(valid, sg, NEG_INF)
                    sk = sink_ref[hh]
                    m = jnp.maximum(jnp.max(sg, axis=-1, keepdims=True), sk)
                    p = jnp.exp(sg - m)
                    den = jnp.sum(p, axis=-1, keepdims=True) + jnp.exp(sk - m)
                    ps.append(p.astype(jnp.bfloat16))
                    rden.append(1.0 / den)
                pcat = jnp.concatenate(ps, axis=0)
                o = jnp.dot(pcat, vv, preferred_element_type=jnp.float32)
                for g in range(GROUP):
                    hh = kh * GROUP + g
                    attn_scr[r0:r0 + ch, hh * HEAD_DIM:(hh + 1) * HEAD_DIM] = o[g * ch:(g + 1) * ch] * rden[g]

    uu = jnp.dot(h, w_in_ref[:, _UA0:_GA0], preferred_element_type=jnp.float32)
    u = uu[:, 0:D_MODEL] * jax.nn.sigmoid(uu[:, D_MODEL:2 * D_MODEL])
    for b in range(nb):
        rows = slice(b * tb, (b + 1) * tb)
        u_scr[b, CONV_HIST:CONV_HIST + tb, :] = u[rows]
        cst_ref[b] = u_scr[b, CONV_HIST + tb - CONV_STATE:CONV_HIST + tb, :]

        shift_len = tb + CONV_HIST - 8
        for s in range(1, 8):
            sh_scr[s - 1, 0:shift_len, :] = u_scr[b, s:s + shift_len, :]

        def conv_rows(r, carry, b=b):
            base = pl.multiple_of(r * CONV_ROWS, CONV_ROWS)
            for j in range(D_MODEL // 128):
                cols = slice(j * 128, (j + 1) * 128)
                acc = jnp.broadcast_to(b_dw_ref[:, cols], (CONV_ROWS, 128))
                for t in range(CONV_WIDTH):
                    off = CONV_HIST - CONV_STATE + t
                    a, s = off // 8, off % 8
                    if s == 0:
                        win = u_scr[b, pl.ds(base + 8 * a, CONV_ROWS), cols]
                    else:
                        win = sh_scr[s - 1, pl.ds(base + 8 * a, CONV_ROWS), cols]
                    acc = acc + w_dw_ref[t:t + 1, cols] * win
                conv_scr[pl.ds(b * tb + base, CONV_ROWS), cols] = acc
            return carry

        lax.fori_loop(0, tb // CONV_ROWS, conv_rows, 0)

    y = conv_scr[...]
    mu = jnp.mean(y, axis=-1, keepdims=True)
    yc = y - mu
    yn = yc * lax.rsqrt(jnp.mean(yc * yc, axis=-1, keepdims=True) + EPS) * ln_g_ref[...] + ln_b_ref[...]
    ya = yn * jax.nn.sigmoid(yn)
    conv = _bf16_dot(ya, w_co_ref[...])

    gl = jnp.dot(h, w_in_ref[:, _GA0:IN_WIDTH], preferred_element_type=jnp.float32)
    gate = jax.nn.sigmoid(gl)
    mix = gate[:, 0:D_MODEL] * attn_scr[...] + gate[:, D_MODEL:2 * D_MODEL] * conv
    proj = _bf16_dot(mix, w_out_ref[...])
    x1 = x + _rms(proj, g_post_ref[...])
    x1_ref[...] = x1.reshape(nb, tb, D_MODEL)

    if not carry_from_inputs:
        for b in range(nb):
            k_scr[b, 0:WINDOW, :] = k_scr[b, tb:tb + WINDOW, :]
            v_scr[b, 0:WINDOW, :] = v_scr[b, tb:tb + WINDOW, :]
            u_scr[b, 0:CONV_HIST, :] = u_scr[b, tb:tb + CONV_HIST, :]


def _const_spec(shape):
    zeros = (0,) * len(shape)
    return pl.BlockSpec(shape, lambda i, j: zeros, pipeline_mode=pl.Buffered(1))


def _mixer(x, hist, table, sink, params, *, nb, tb, ch):
    B, S, _ = x.shape
    carry_from_inputs = hist is not None
    nkeys = WINDOW + ch
    rel = jnp.arange(nkeys)[None, :] - WINDOW - jnp.arange(ch)[:, None]
    bucket = _t5_bucket(rel).astype(jnp.int32)
    g_pre, w_in, w_dw, b_dw, ln_g, ln_b, w_co, w_out, g_post = params

    grid = (B // nb, S // tb)
    smem = pl.BlockSpec(memory_space=pltpu.SMEM)
    in_specs = [smem, smem, _const_spec((ch, nkeys)),
                pl.BlockSpec((nb, tb, D_MODEL), lambda i, j: (i, j, 0))]
    args = [table, sink, bucket, x]
    if carry_from_inputs:
        in_specs += [pl.BlockSpec((nb, WINDOW, KV_WIDTH), lambda i, j: (i, 0, 0)),
                     pl.BlockSpec((nb, WINDOW, KV_WIDTH), lambda i, j: (i, 0, 0)),
                     pl.BlockSpec((nb, CONV_STATE, D_MODEL), lambda i, j: (i, 0, 0))]
        args += list(hist)
    weights = [g_pre, w_in, w_dw, b_dw, ln_g, ln_b, w_co, w_out, g_post]
    in_specs += [_const_spec(w.shape) for w in weights]
    args += weights

    out_shape = (jax.ShapeDtypeStruct((B, S, D_MODEL), jnp.float32),
                 jax.ShapeDtypeStruct((B, WINDOW, KV_WIDTH), jnp.float32),
                 jax.ShapeDtypeStruct((B, WINDOW, KV_WIDTH), jnp.float32),
                 jax.ShapeDtypeStruct((B, CONV_STATE, D_MODEL), jnp.float32))
    out_specs = (pl.BlockSpec((nb, tb, D_MODEL), lambda i, j: (i, j, 0)),
                 pl.BlockSpec((nb, WINDOW, KV_WIDTH), lambda i, j: (i, 0, 0)),
                 pl.BlockSpec((nb, WINDOW, KV_WIDTH), lambda i, j: (i, 0, 0)),
                 pl.BlockSpec((nb, CONV_STATE, D_MODEL), lambda i, j: (i, 0, 0)))
    scratch = [pltpu.VMEM((N_HEADS, ch, nkeys), jnp.float32),
               pltpu.VMEM((nb, WINDOW + tb, KV_WIDTH), jnp.bfloat16),
               pltpu.VMEM((nb, WINDOW + tb, KV_WIDTH), jnp.bfloat16),
               pltpu.VMEM((nb, CONV_HIST + tb, D_MODEL), jnp.float32),
               pltpu.VMEM((nb * tb, D_MODEL), jnp.float32),
               pltpu.VMEM((nb * tb, D_MODEL), jnp.float32),
               pltpu.VMEM((7, CONV_HIST + tb - 8, D_MODEL), jnp.float32)]
    kern = functools.partial(_mixer_kernel, nb=nb, tb=tb, ch=ch, carry_from_inputs=carry_from_inputs)
    return pl.pallas_call(
        kern, grid=grid, in_specs=in_specs, out_specs=out_specs, out_shape=out_shape,
        scratch_shapes=scratch,
        compiler_params=pltpu.CompilerParams(dimension_semantics=("arbitrary", "arbitrary"),
                                             vmem_limit_bytes=VMEM_LIMIT_BYTES),
        name="mixer_sample" if carry_from_inputs else "mixer_prompt",
    )(*args)


def _ffn_kernel(x_ref, g_pre_ref, wg_ref, wu_ref, wd_ref, g_post_ref, y_ref):
    x = x_ref[...]
    h = _rms(x, g_pre_ref[...]).astype(jnp.bfloat16)
    a = jnp.dot(h, wg_ref[...], preferred_element_type=jnp.float32)
    b = jnp.dot(h, wu_ref[...], preferred_element_type=jnp.float32)
    act = a * jax.nn.sigmoid(a) * b
    f = _bf16_dot(act, wd_ref[...])
    y_ref[...] = x + _rms(f, g_post_ref[...])


def _ffn(x, params, *, tm, name):
    n = x.shape[0]
    g_pre, wg, wu, wd, g_post = params
    const = lambda shape: pl.BlockSpec(shape, lambda i: (0,) * len(shape), pipeline_mode=pl.Buffered(1))
    return pl.pallas_call(
        _ffn_kernel, grid=(n // tm,),
        in_specs=[pl.BlockSpec((tm, D_MODEL), lambda i: (i, 0))] + [const(w.shape) for w in params],
        out_specs=pl.BlockSpec((tm, D_MODEL), lambda i: (i, 0)),
        out_shape=jax.ShapeDtypeStruct((n, D_MODEL), jnp.float32),
        compiler_params=pltpu.CompilerParams(dimension_semantics=("arbitrary",),
                                             vmem_limit_bytes=VMEM_LIMIT_BYTES),
        name=name,
    )(x, *params)


def kernel(x_prompt, x_sample, cache_k, cache_v, state_conv, rel_bias_table, attn_sink, g_pre_mix, w_in, w_dw, b_dw, ln_g, ln_b, w_conv_out, w_out, g_post_mix, g_pre_ffn, w_ffn_gate, w_ffn_up, w_ffn_down, g_post_ffn):
    depth = w_in.shape[0]
    assert depth == 1, "single-layer step"
    bf = jnp.bfloat16
    row = lambda a: a.reshape(1, -1)
    l = 0
    mixer_params = (row(g_pre_mix[l]), w_in[l].astype(bf), w_dw[l], row(b_dw[l]), row(ln_g[l]), row(ln_b[l]),
                    w_conv_out[l].astype(bf), w_out[l].astype(bf), row(g_post_mix[l]))
    ffn_params = (row(g_pre_ffn[l]), w_ffn_gate[l].astype(bf), w_ffn_up[l].astype(bf),
                  w_ffn_down[l].astype(bf), row(g_post_ffn[l]))
    sink = attn_sink[l]

    B, S, _ = x_prompt.shape
    xp1, kp, vp, cp = _mixer(x_prompt, None, rel_bias_table, sink, mixer_params, nb=1, tb=256, ch=CHUNK)
    yp = _ffn(xp1.reshape(B * S, D_MODEL), ffn_params, tm=512, name="ffn_prompt").reshape(B, S, D_MODEL)

    Bs, Ss, _ = x_sample.shape
    hist = (cache_k[l].reshape(Bs, WINDOW, KV_WIDTH), cache_v[l].reshape(Bs, WINDOW, KV_WIDTH), state_conv[l])
    xs1, ks, vs, cs = _mixer(x_sample, hist, rel_bias_table, sink, mixer_params, nb=8, tb=Ss, ch=Ss)
    ys = _ffn(xs1.reshape(Bs * Ss, D_MODEL), ffn_params, tm=512, name="ffn_sample").reshape(Bs, Ss, D_MODEL)

    kv5 = lambda a: a.reshape(1, a.shape[0], WINDOW, N_KV_HEADS, HEAD_DIM)
    return (yp, ys, kv5(kp), kv5(vp), cp[None], kv5(ks), kv5(vs), cs[None])
```

```python
import functools
import math

import jax
import jax.numpy as jnp
from jax import lax
from jax.experimental import pallas as pl
from jax.experimental.pallas import tpu as pltpu

D_MODEL = 1024
HEAD_DIM = 64
N_HEADS = 16
N_KV_HEADS = 2
GROUP = N_HEADS // N_KV_HEADS
KV_WIDTH = N_KV_HEADS * HEAD_DIM
WINDOW = 128
CHUNK = 64
CONV_WIDTH = 31
CONV_STATE = CONV_WIDTH - 1
CONV_HIST = 32
CONV_ROWS = 32
N_COL_BLOCKS = D_MODEL // 128
ROW_BLOCK = 16
D_FF = 2816
N_BUCKETS = 32
MAX_DISTANCE = 128
EPS = 1e-6
NEG_INF = -1e30
IN_WIDTH = D_MODEL + 2 * KV_WIDTH + 2 * D_MODEL + 2 * D_MODEL

_Q0 = 0
_K0 = D_MODEL
_V0 = _K0 + KV_WIDTH
_UA0 = _V0 + KV_WIDTH
_UB0 = _UA0 + D_MODEL
_GA0 = _UB0 + D_MODEL
_GB0 = _GA0 + D_MODEL

VMEM_LIMIT_BYTES = 56 * 1024 * 1024


def _t5_bucket(rel):
    half = N_BUCKETS // 2
    max_exact = half // 2
    ret = jnp.where(rel > 0, half, 0)
    n = jnp.abs(rel)
    nf = jnp.maximum(n, 1).astype(jnp.float32)
    large = max_exact + (jnp.log(nf / max_exact) / math.log(MAX_DISTANCE / max_exact)
                         * (half - max_exact)).astype(jnp.int32)
    large = jnp.minimum(large, half - 1)
    return ret + jnp.where(n < max_exact, n, large)


def _rms(x, g):
    return x * lax.rsqrt(jnp.mean(x * x, axis=-1, keepdims=True) + EPS) * g


def _bf16_dot(a, w):
    return jnp.dot(a.astype(jnp.bfloat16), w, preferred_element_type=jnp.float32)


def _mixer_kernel(*refs, nb, tb, ch, carry_from_inputs):
    it = iter(refs)
    table_ref = next(it)
    sink_ref = next(it)
    bucket_ref = next(it)
    x_ref = next(it)
    if carry_from_inputs:
        ck_ref = next(it)
        cv_ref = next(it)
        cs_ref = next(it)
    g_pre_ref = next(it)
    w_in_ref = next(it)
    w_dw_ref = next(it)
    b_dw_ref = next(it)
    ln_g_ref = next(it)
    ln_b_ref = next(it)
    w_co_ref = next(it)
    w_out_ref = next(it)
    g_post_ref = next(it)
    x1_ref = next(it)
    kwin_ref = next(it)
    vwin_ref = next(it)
    cst_ref = next(it)
    bias_scr = next(it)
    k_scr = next(it)
    v_scr = next(it)
    u_scr = next(it)
    attn_scr = next(it)
    conv_scr = next(it)
    convo_scr = next(it)
    gate_scr = next(it)
    h_scr = next(it)
    q_scr = next(it)
    ya_scr = next(it)
    mix_scr = next(it)

    nkeys = WINDOW + ch
    n_chunks = tb // ch
    first_step = (pl.program_id(0) == 0) & (pl.program_id(1) == 0)
    seq_tile = pl.program_id(1)

    @pl.when(first_step)
    def _build_bias():
        bucket = bucket_ref[...]
        bias_scr[...] = jnp.zeros_like(bias_scr)

        def body(b, carry):
            hit = bucket == b
            for h in range(N_HEADS):
                bias_scr[h] = jnp.where(hit, table_ref[b, h], bias_scr[h])
            return carry

        lax.fori_loop(0, N_BUCKETS, body, 0)

    if carry_from_inputs:
        for b in range(nb):
            k_scr[b, 0:WINDOW, :] = ck_ref[b].astype(jnp.bfloat16)
            v_scr[b, 0:WINDOW, :] = cv_ref[b].astype(jnp.bfloat16)
            for j in range(N_COL_BLOCKS):
                u_scr[b, j, 0:8, :] = jnp.zeros((8, 128), jnp.float32)
                u_scr[b, j, CONV_HIST - CONV_STATE:CONV_HIST, :] = cs_ref[b, :, j * 128:(j + 1) * 128]
    else:
        @pl.when(seq_tile == 0)
        def _reset_history():
            for b in range(nb):
                k_scr[b, 0:WINDOW, :] = jnp.zeros((WINDOW, KV_WIDTH), jnp.bfloat16)
                v_scr[b, 0:WINDOW, :] = jnp.zeros((WINDOW, KV_WIDTH), jnp.bfloat16)
                u_scr[b, :, 0:CONV_HIST, :] = jnp.zeros((N_COL_BLOCKS, CONV_HIST, 128), jnp.float32)

    m_rows = nb * tb
    f32, bf16 = jnp.float32, jnp.bfloat16

    def row_blocks():
        for r0 in range(0, m_rows, ROW_BLOCK):
            b, t0 = divmod(r0, tb)
            yield r0, b, t0

    for r0, b, t0 in row_blocks():
        xb = x_ref[b, t0:t0 + ROW_BLOCK, :]
        h_scr[r0:r0 + ROW_BLOCK, :] = _rms(xb, g_pre_ref[...]).astype(bf16)

    def proj(c0, n):
        return jnp.dot(h_scr[...], w_in_ref[:, c0:c0 + n], preferred_element_type=f32)

    def do_q(i):
        c0 = i * 512
        q_scr[:, c0:c0 + 512] = (proj(_Q0 + c0, 512) * (HEAD_DIM ** -0.5)).astype(bf16)

    def do_kv():
        kv = proj(_K0, 2 * KV_WIDTH)
        k = kv[:, 0:KV_WIDTH]
        v = kv[:, KV_WIDTH:2 * KV_WIDTH]
        for b in range(nb):
            rows = slice(b * tb, (b + 1) * tb)
            k_scr[b, WINDOW:WINDOW + tb, :] = k[rows].astype(bf16)
            v_scr[b, WINDOW:WINDOW + tb, :] = v[rows].astype(bf16)
            if tb >= WINDOW:
                kwin_ref[b] = k[b * tb + tb - WINDOW:(b + 1) * tb]
                vwin_ref[b] = v[b * tb + tb - WINDOW:(b + 1) * tb]
            else:
                kwin_ref[b, 0:WINDOW - tb, :] = ck_ref[b, tb:WINDOW, :]
                vwin_ref[b, 0:WINDOW - tb, :] = cv_ref[b, tb:WINDOW, :]
                kwin_ref[b, WINDOW - tb:WINDOW, :] = k[rows]
                vwin_ref[b, WINDOW - tb:WINDOW, :] = v[rows]

    def do_gate(i):
        c0 = i * 256
        gate_scr[:, c0:c0 + 256] = jax.nn.sigmoid(proj(_GA0 + c0, 256))

    side_tasks = [functools.partial(do_q, 0), functools.partial(do_q, 1), do_kv]
    side_tasks += [functools.partial(do_gate, i) for i in range(2 * D_MODEL // 256)]

    def glu_chunk(i):
        c0 = i * 256
        ua = proj(_UA0 + c0, 256)
        ub = proj(_UB0 + c0, 256)
        return ua * jax.nn.sigmoid(ub)

    n_conv_chunks = D_MODEL // 256
    units_per_chunk = nb * (tb // CONV_ROWS)
    total_units = n_conv_chunks * units_per_chunk
    side_after = [((k + 1) * total_units) // (len(side_tasks) + 1) for k in range(len(side_tasks))]
    next_side = 0
    unit = 0
    u_next = glu_chunk(0)
    for i in range(n_conv_chunks):
        u_cur = u_next
        for jj in range(2):
            j = 2 * i + jj
            for b in range(nb):
                u_scr[b, j, CONV_HIST:CONV_HIST + tb, :] = u_cur[b * tb:(b + 1) * tb, jj * 128:(jj + 1) * 128]
                cst_ref[b, :, j * 128:(j + 1) * 128] = u_scr[b, j, CONV_HIST + tb - CONV_STATE:CONV_HIST + tb, :]
        if i + 1 < n_conv_chunks:
            u_next = glu_chunk(i + 1)
        for b in range(nb):
            for r in range(tb // CONV_ROWS):
                base = r * CONV_ROWS
                for jj in range(2):
                    j = 2 * i + jj
                    cols = slice(j * 128, (j + 1) * 128)
                    acc = jnp.broadcast_to(b_dw_ref[:, cols], (CONV_ROWS, 128))
                    for t in range(CONV_WIDTH):
                        off = base + CONV_HIST - CONV_STATE + t
                        acc = acc + w_dw_ref[t:t + 1, cols] * u_scr[b, j, off:off + CONV_ROWS, :]
                    conv_scr[b * tb + base:b * tb + base + CONV_ROWS, cols] = acc
                unit += 1
                while next_side < len(side_tasks) and side_after[next_side] <= unit:
                    side_tasks[next_side]()
                    next_side += 1
    while next_side < len(side_tasks):
        side_tasks[next_side]()
        next_side += 1

    key_iota = lax.broadcasted_iota(jnp.int32, (ch, nkeys), 1)
    groups = [(b, c, kh) for b in range(nb) for c in range(n_chunks) for kh in range(N_KV_HEADS)]

    def scores(b, c, kh):
        r0 = b * tb + c * ch
        kk = k_scr[b, c * ch:c * ch + nkeys, kh * HEAD_DIM:(kh + 1) * HEAD_DIM]
        qs = jnp.concatenate(
            [q_scr[r0:r0 + ch, (kh * GROUP + g) * HEAD_DIM:(kh * GROUP + g + 1) * HEAD_DIM]
             for g in range(GROUP)], axis=0)
        return lax.dot_general(qs, kk, (((1,), (1,)), ((), ())), preferred_element_type=f32)

    def softmax_pv(b, c, kh, s):
        r0 = b * tb + c * ch
        need_mask = (not carry_from_inputs) and (c * ch < WINDOW)
        if need_mask:
            first_pos = seq_tile * tb + (c * ch - WINDOW)
            valid = (key_iota + first_pos) >= 0
        ps = []
        rden = []
        for g in range(GROUP):
            hh = kh * GROUP + g
            sg = s[g * ch:(g + 1) * ch] + bias_scr[hh]
            if need_mask:
                sg = jnp.where(valid, sg, NEG_INF)
            sk = sink_ref[hh]
            m = jnp.maximum(jnp.max(sg, axis=-1, keepdims=True), sk)
            p = jnp.exp(sg - m)
            den = jnp.sum(p, axis=-1, keepdims=True) + jnp.exp(sk - m)
            ps.append(p.astype(bf16))
            rden.append(1.0 / den)
        pcat = jnp.concatenate(ps, axis=0)
        vv = v_scr[b, c * ch:c * ch + nkeys, kh * HEAD_DIM:(kh + 1) * HEAD_DIM]
        o = jnp.dot(pcat, vv, preferred_element_type=f32)
        for g in range(GROUP):
            hh = kh * GROUP + g
            attn_scr[r0:r0 + ch, hh * HEAD_DIM:(hh + 1) * HEAD_DIM] = o[g * ch:(g + 1) * ch] * rden[g]

    s_next = scores(*groups[0])
    for r0, b, t0 in row_blocks():
        y = conv_scr[r0:r0 + ROW_BLOCK, :]
        mu = jnp.mean(y, axis=-1, keepdims=True)
        yc = y - mu
        yn = yc * lax.rsqrt(jnp.mean(yc * yc, axis=-1, keepdims=True) + EPS) * ln_g_ref[...] + ln_b_ref[...]
        ya_scr[r0:r0 + ROW_BLOCK, :] = (yn * jax.nn.sigmoid(yn)).astype(bf16)

    n_co_chunks = D_MODEL // 256
    co_after = [((k + 1) * len(groups)) // (n_co_chunks + 1) for k in range(n_co_chunks)]
    next_co = 0
    for gi, grp in enumerate(groups):
        s_cur = s_next
        if gi + 1 < len(groups):
            s_next = scores(*groups[gi + 1])
        softmax_pv(*grp, s_cur)
        while next_co < n_co_chunks and co_after[next_co] <= gi + 1:
            c0 = next_co * 256
            convo_scr[:, c0:c0 + 256] = jnp.dot(ya_scr[...], w_co_ref[:, c0:c0 + 256], preferred_element_type=f32)
            next_co += 1

    for r0, b, t0 in row_blocks():
        rows = slice(r0, r0 + ROW_BLOCK)
        mix = gate_scr[rows, 0:D_MODEL] * attn_scr[rows, :] + gate_scr[rows, D_MODEL:2 * D_MODEL] * convo_scr[rows, :]
        mix_scr[rows, :] = mix.astype(bf16)
    for c0 in range(0, D_MODEL, 256):
        conv_scr[:, c0:c0 + 256] = jnp.dot(mix_scr[...], w_out_ref[:, c0:c0 + 256], preferred_element_type=f32)
    for r0, b, t0 in row_blocks():
        xb = x_ref[b, t0:t0 + ROW_BLOCK, :]
        x1_ref[b, t0:t0 + ROW_BLOCK, :] = xb + _rms(conv_scr[r0:r0 + ROW_BLOCK, :], g_post_ref[...])

    if not carry_from_inputs:
        for b in range(nb):
            k_scr[b, 0:WINDOW, :] = k_scr[b, tb:tb + WINDOW, :]
            v_scr[b, 0:WINDOW, :] = v_scr[b, tb:tb + WINDOW, :]
            u_scr[b, :, 0:CONV_HIST, :] = u_scr[b, :, tb:tb + CONV_HIST, :]


def _const_spec(shape):
    zeros = (0,) * len(shape)
    return pl.BlockSpec(shape, lambda i, j: zeros, pipeline_mode=pl.Buffered(1))


def _mixer(x, hist, table, sink, params, *, nb, tb, ch):
    B, S, _ = x.shape
    carry_from_inputs = hist is not None
    nkeys = WINDOW + ch
    rel = jnp.arange(nkeys)[None, :] - WINDOW - jnp.arange(ch)[:, None]
    bucket = _t5_bucket(rel).astype(jnp.int32)
    g_pre, w_in, w_dw, b_dw, ln_g, ln_b, w_co, w_out, g_post = params

    grid = (B // nb, S // tb)
    smem = pl.BlockSpec(memory_space=pltpu.SMEM)
    in_specs = [smem, smem, _const_spec((ch, nkeys)),
                pl.BlockSpec((nb, tb, D_MODEL), lambda i, j: (i, j, 0))]
    args = [table, sink, bucket, x]
    if carry_from_inputs:
        in_specs += [pl.BlockSpec((nb, WINDOW, KV_WIDTH), lambda i, j: (i, 0, 0)),
                     pl.BlockSpec((nb, WINDOW, KV_WIDTH), lambda i, j: (i, 0, 0)),
                     pl.BlockSpec((nb, CONV_STATE, D_MODEL), lambda i, j: (i, 0, 0))]
        args += list(hist)
    weights = [g_pre, w_in, w_dw, b_dw, ln_g, ln_b, w_co, w_out, g_post]
    in_specs += [_const_spec(w.shape) for w in weights]
    args += weights

    out_shape = (jax.ShapeDtypeStruct((B, S, D_MODEL), jnp.float32),
                 jax.ShapeDtypeStruct((B, WINDOW, KV_WIDTH), jnp.float32),
                 jax.ShapeDtypeStruct((B, WINDOW, KV_WIDTH), jnp.float32),
                 jax.ShapeDtypeStruct((B, CONV_STATE, D_MODEL), jnp.float32))
    out_specs = (pl.BlockSpec((nb, tb, D_MODEL), lambda i, j: (i, j, 0)),
                 pl.BlockSpec((nb, WINDOW, KV_WIDTH), lambda i, j: (i, 0, 0)),
                 pl.BlockSpec((nb, WINDOW, KV_WIDTH), lambda i, j: (i, 0, 0)),
                 pl.BlockSpec((nb, CONV_STATE, D_MODEL), lambda i, j: (i, 0, 0)))
    scratch = [pltpu.VMEM((N_HEADS, ch, nkeys), jnp.float32),
               pltpu.VMEM((nb, WINDOW + tb, KV_WIDTH), jnp.bfloat16),
               pltpu.VMEM((nb, WINDOW + tb, KV_WIDTH), jnp.bfloat16),
               pltpu.VMEM((nb, N_COL_BLOCKS, CONV_HIST + tb, 128), jnp.float32),
               pltpu.VMEM((nb * tb, D_MODEL), jnp.float32),
               pltpu.VMEM((nb * tb, D_MODEL), jnp.float32),
               pltpu.VMEM((nb * tb, D_MODEL), jnp.float32),
               pltpu.VMEM((nb * tb, 2 * D_MODEL), jnp.float32),
               pltpu.VMEM((nb * tb, D_MODEL), jnp.bfloat16),
               pltpu.VMEM((nb * tb, D_MODEL), jnp.bfloat16),
               pltpu.VMEM((nb * tb, D_MODEL), jnp.bfloat16),
               pltpu.VMEM((nb * tb, D_MODEL), jnp.bfloat16)]
    kern = functools.partial(_mixer_kernel, nb=nb, tb=tb, ch=ch, carry_from_inputs=carry_from_inputs)
    return pl.pallas_call(
        kern, grid=grid, in_specs=in_specs, out_specs=out_specs, out_shape=out_shape,
        scratch_shapes=scratch,
        compiler_params=pltpu.CompilerParams(dimension_semantics=("arbitrary", "arbitrary"),
                                             vmem_limit_bytes=VMEM_LIMIT_BYTES),
        name="mixer_sample" if carry_from_inputs else "mixer_prompt",
    )(*args)


def _ffn_kernel(x_ref, g_pre_ref, wg_ref, wu_ref, wd_ref, g_post_ref, y_ref):
    x = x_ref[...]
    h = _rms(x, g_pre_ref[...]).astype(jnp.bfloat16)
    a = jnp.dot(h, wg_ref[...], preferred_element_type=jnp.float32)
    b = jnp.dot(h, wu_ref[...], preferred_element_type=jnp.float32)
    act = a * jax.nn.sigmoid(a) * b
    f = _bf16_dot(act, wd_ref[...])
    y_ref[...] = x + _rms(f, g_post_ref[...])


def _ffn(x, params, *, tm, name):
    n = x.shape[0]
    g_pre, wg, wu, wd, g_post = params
    const = lambda shape: pl.BlockSpec(shape, lambda i: (0,) * len(shape), pipeline_mode=pl.Buffered(1))
    return pl.pallas_call(
        _ffn_kernel, grid=(n // tm,),
        in_specs=[pl.BlockSpec((tm, D_MODEL), lambda i: (i, 0))] + [const(w.shape) for w in params],
        out_specs=pl.BlockSpec((tm, D_MODEL), lambda i: (i, 0)),
        out_shape=jax.ShapeDtypeStruct((n, D_MODEL), jnp.float32),
        compiler_params=pltpu.CompilerParams(dimension_semantics=("arbitrary",),
                                             vmem_limit_bytes=VMEM_LIMIT_BYTES),
        name=name,
    )(x, *params)


def kernel(x_prompt, x_sample, cache_k, cache_v, state_conv, rel_bias_table, attn_sink, g_pre_mix, w_in, w_dw, b_dw, ln_g, ln_b, w_conv_out, w_out, g_post_mix, g_pre_ffn, w_ffn_gate, w_ffn_up, w_ffn_down, g_post_ffn):
    depth = w_in.shape[0]
    assert depth == 1, "single-layer step"
    bf = jnp.bfloat16
    row = lambda a: a.reshape(1, -1)
    l = 0
    mixer_params = (row(g_pre_mix[l]), w_in[l].astype(bf), w_dw[l], row(b_dw[l]), row(ln_g[l]), row(ln_b[l]),
                    w_conv_out[l].astype(bf), w_out[l].astype(bf), row(g_post_mix[l]))
    ffn_params = (row(g_pre_ffn[l]), w_ffn_gate[l].astype(bf), w_ffn_up[l].astype(bf),
                  w_ffn_down[l].astype(bf), row(g_post_ffn[l]))
    sink = attn_sink[l]

    B, S, _ = x_prompt.shape
    xp1, kp, vp, cp = _mixer(x_prompt, None, rel_bias_table, sink, mixer_params, nb=1, tb=256, ch=CHUNK)
    yp = _ffn(xp1.reshape(B * S, D_MODEL), ffn_params, tm=512, name="ffn_prompt").reshape(B, S, D_MODEL)

    Bs, Ss, _ = x_sample.shape
    hist = (cache_k[l].reshape(Bs, WINDOW, KV_WIDTH), cache_v[l].reshape(Bs, WINDOW, KV_WIDTH), state_conv[l])
    xs1, ks, vs, cs = _mixer(x_sample, hist, rel_bias_table, sink, mixer_params, nb=8, tb=Ss, ch=Ss)
    ys = _ffn(xs1.reshape(Bs * Ss, D_MODEL), ffn_params, tm=512, name="ffn_sample").reshape(Bs, Ss, D_MODEL)

    kv5 = lambda a: a.reshape(1, a.shape[0], WINDOW, N_KV_HEADS, HEAD_DIM)
    return (yp, ys, kv5(kp), kv5(vp), cp[None], kv5(ks), kv5(vs), cs[None])
```

```python
import functools
import math

import jax
import jax.numpy as jnp
from jax import lax
from jax.experimental import pallas as pl
from jax.experimental.pallas import tpu as pltpu

D_MODEL = 1024
HEAD_DIM = 64
N_HEADS = 16
N_KV_HEADS = 2
GROUP = N_HEADS // N_KV_HEADS
PAIRS = GROUP // 2
KV_WIDTH = N_KV_HEADS * HEAD_DIM
WINDOW = 128
CHUNK = 64
KEY_PAD = 256
CONV_WIDTH = 31
CONV_STATE = CONV_WIDTH - 1
CONV_HIST = 32
CONV_ROWS = 32
N_COL_BLOCKS = D_MODEL // 128
ROW_BLOCK = 16
D_FF = 2816
N_BUCKETS = 32
MAX_DISTANCE = 128
EPS = 1e-6
NEG_INF = -1e30
LOG2E = math.log2(math.e)
IN_WIDTH = D_MODEL + 2 * KV_WIDTH + 2 * D_MODEL + 2 * D_MODEL

_Q0 = 0
_K0 = D_MODEL
_V0 = _K0 + KV_WIDTH
_UA0 = _V0 + KV_WIDTH
_UB0 = _UA0 + D_MODEL
_GA0 = _UB0 + D_MODEL

VMEM_LIMIT_BYTES = 56 * 1024 * 1024


def _t5_bucket(rel):
    half = N_BUCKETS // 2
    max_exact = half // 2
    ret = jnp.where(rel > 0, half, 0)
    n = jnp.abs(rel)
    nf = jnp.maximum(n, 1).astype(jnp.float32)
    large = max_exact + (jnp.log(nf / max_exact) / math.log(MAX_DISTANCE / max_exact)
                         * (half - max_exact)).astype(jnp.int32)
    large = jnp.minimum(large, half - 1)
    return ret + jnp.where(n < max_exact, n, large)


def _rms(x, g):
    return x * lax.rsqrt(jnp.mean(x * x, axis=-1, keepdims=True) + EPS) * g


def _bf16_dot(a, w):
    return jnp.dot(a.astype(jnp.bfloat16), w, preferred_element_type=jnp.float32)


def _mixer_kernel(*refs, nb, tb, ch, carry_from_inputs):
    it = iter(refs)
    table_ref = next(it)
    sink_ref = next(it)
    bucket_ref = next(it)
    x_ref = next(it)
    if carry_from_inputs:
        ck_ref = next(it)
        cv_ref = next(it)
        cs_ref = next(it)
    g_pre_ref = next(it)
    w_in_ref = next(it)
    w_dw_ref = next(it)
    b_dw_ref = next(it)
    ln_g_ref = next(it)
    ln_b_ref = next(it)
    w_co_ref = next(it)
    w_out_ref = next(it)
    g_post_ref = next(it)
    x1_ref = next(it)
    kwin_ref = next(it)
    vwin_ref = next(it)
    cst_ref = next(it)
    bias_scr = next(it)
    kx_scr = next(it)
    vx_scr = next(it)
    u_scr = next(it)
    attn_scr = next(it)
    conv_scr = next(it)
    convo_scr = next(it)
    gate_scr = next(it)
    h_scr = next(it)
    q_scr = next(it)
    ya_scr = next(it)
    mix_scr = next(it)
    proj_scr = next(it)

    f32, bf16 = jnp.float32, jnp.bfloat16
    n_chunks = tb // ch
    key_rows = kx_scr.shape[3]
    first_step = (pl.program_id(0) == 0) & (pl.program_id(1) == 0)
    seq_tile = pl.program_id(1)
    low_lanes = lax.broadcasted_iota(jnp.int32, (1, KV_WIDTH), 1) < HEAD_DIM

    @pl.when(first_step)
    def _first_step():
        bucket = bucket_ref[...]
        pad_bias = jnp.where(bucket < 0, NEG_INF, 0.0)
        for h in range(N_HEADS):
            bias_scr[h] = pad_bias

        def body(b, carry):
            hit = bucket == b
            for h in range(N_HEADS):
                bias_scr[h] = jnp.where(hit, table_ref[b, h] * LOG2E, bias_scr[h])
            return carry

        lax.fori_loop(0, N_BUCKETS, body, 0)
        kx_scr[...] = jnp.zeros_like(kx_scr)
        vx_scr[:, :, :, :, 0:KV_WIDTH] = jnp.zeros(vx_scr.shape[:4] + (KV_WIDTH,), bf16)
        vx_scr[:, :, :, :, KV_WIDTH:2 * KV_WIDTH] = jnp.ones(vx_scr.shape[:4] + (KV_WIDTH,), bf16)

    def store_kv(b, row0, k, v):
        n = k.shape[0]
        k_sw = pltpu.roll(k, HEAD_DIM, axis=1)
        v_sw = pltpu.roll(v, HEAD_DIM, axis=1)
        zero = jnp.zeros((), f32)
        for kh, (same, swapped) in enumerate(((k, k_sw), (k_sw, k))):
            kx_scr[b, kh, 0, row0:row0 + n, :] = jnp.where(low_lanes, same, zero).astype(bf16)
            kx_scr[b, kh, 1, row0:row0 + n, :] = jnp.where(low_lanes, zero, swapped).astype(bf16)
        for kh, (same, swapped) in enumerate(((v, v_sw), (v_sw, v))):
            vx_scr[b, kh, 0, row0:row0 + n, 0:KV_WIDTH] = jnp.where(low_lanes, same, zero).astype(bf16)
            vx_scr[b, kh, 1, row0:row0 + n, 0:KV_WIDTH] = jnp.where(low_lanes, zero, swapped).astype(bf16)

    if carry_from_inputs:
        for b in range(nb):
            store_kv(b, 0, ck_ref[b], cv_ref[b])
            for j in range(N_COL_BLOCKS):
                u_scr[b, j, 0:8, :] = jnp.zeros((8, 128), f32)
                u_scr[b, j, CONV_HIST - CONV_STATE:CONV_HIST, :] = cs_ref[b, :, j * 128:(j + 1) * 128]
    else:
        @pl.when(seq_tile == 0)
        def _reset_history():
            for b in range(nb):
                for kh in range(N_KV_HEADS):
                    for par in range(2):
                        kx_scr[b, kh, par, 0:WINDOW, :] = jnp.zeros((WINDOW, KV_WIDTH), bf16)
                        vx_scr[b, kh, par, 0:WINDOW, 0:KV_WIDTH] = jnp.zeros((WINDOW, KV_WIDTH), bf16)
                u_scr[b, :, 0:CONV_HIST, :] = jnp.zeros((N_COL_BLOCKS, CONV_HIST, 128), f32)

        @pl.when(seq_tile != 0)
        def _roll_history():
            for b in range(nb):
                for kh in range(N_KV_HEADS):
                    for par in range(2):
                        kx_scr[b, kh, par, 0:WINDOW, :] = kx_scr[b, kh, par, tb:tb + WINDOW, :]
                        vx_scr[b, kh, par, 0:WINDOW, 0:KV_WIDTH] = vx_scr[b, kh, par, tb:tb + WINDOW, 0:KV_WIDTH]
                u_scr[b, :, 0:CONV_HIST, :] = u_scr[b, :, tb:tb + CONV_HIST, :]

    m_rows = nb * tb

    def row_blocks():
        for r0 in range(0, m_rows, ROW_BLOCK):
            b, t0 = divmod(r0, tb)
            yield r0, b, t0

    for r0, b, t0 in row_blocks():
        xb = x_ref[b, t0:t0 + ROW_BLOCK, :]
        h_scr[r0:r0 + ROW_BLOCK, :] = _rms(xb, g_pre_ref[...]).astype(bf16)

    def proj(c0, n):
        return jnp.dot(h_scr[...], w_in_ref[:, c0:c0 + n], preferred_element_type=f32)

    def do_q(i):
        c0 = i * 512
        q_scr[:, c0:c0 + 512] = (proj(_Q0 + c0, 512) * (HEAD_DIM ** -0.5 * LOG2E)).astype(bf16)

    def do_kv():
        kv = proj(_K0, 2 * KV_WIDTH)
        k = kv[:, 0:KV_WIDTH]
        v = kv[:, KV_WIDTH:2 * KV_WIDTH]
        for b in range(nb):
            rows = slice(b * tb, (b + 1) * tb)
            store_kv(b, WINDOW, k[rows], v[rows])
            if tb >= WINDOW:
                kwin_ref[b] = k[b * tb + tb - WINDOW:(b + 1) * tb]
                vwin_ref[b] = v[b * tb + tb - WINDOW:(b + 1) * tb]
            else:
                kwin_ref[b, 0:WINDOW - tb, :] = ck_ref[b, tb:WINDOW, :]
                vwin_ref[b, 0:WINDOW - tb, :] = cv_ref[b, tb:WINDOW, :]
                kwin_ref[b, WINDOW - tb:WINDOW, :] = k[rows]
                vwin_ref[b, WINDOW - tb:WINDOW, :] = v[rows]

    def do_gate(i):
        c0 = i * 256
        gate_scr[:, c0:c0 + 256] = jax.nn.sigmoid(proj(_GA0 + c0, 256))

    side_tasks = [functools.partial(do_q, 0), functools.partial(do_q, 1), do_kv]
    side_tasks += [functools.partial(do_gate, i) for i in range(2 * D_MODEL // 256)]

    def glu_chunk(i):
        c0 = i * 256
        ua = proj(_UA0 + c0, 256)
        ub = proj(_UB0 + c0, 256)
        return ua * jax.nn.sigmoid(ub)

    n_conv_chunks = D_MODEL // 256
    units_per_chunk = nb * (tb // CONV_ROWS)
    total_units = n_conv_chunks * units_per_chunk
    side_after = [((k + 1) * total_units) // (len(side_tasks) + 1) for k in range(len(side_tasks))]
    next_side = 0
    unit = 0
    u_next = glu_chunk(0)
    for i in range(n_conv_chunks):
        u_cur = u_next
        for jj in range(2):
            j = 2 * i + jj
            for b in range(nb):
                u_scr[b, j, CONV_HIST:CONV_HIST + tb, :] = u_cur[b * tb:(b + 1) * tb, jj * 128:(jj + 1) * 128]
                cst_ref[b, :, j * 128:(j + 1) * 128] = u_scr[b, j, CONV_HIST + tb - CONV_STATE:CONV_HIST + tb, :]
        if i + 1 < n_conv_chunks:
            u_next = glu_chunk(i + 1)
        for b in range(nb):
            for r in range(tb // CONV_ROWS):
                base = r * CONV_ROWS
                for jj in range(2):
                    j = 2 * i + jj
                    cols = slice(j * 128, (j + 1) * 128)
                    acc = jnp.broadcast_to(b_dw_ref[:, cols], (CONV_ROWS, 128))
                    for t in range(CONV_WIDTH):
                        off = base + CONV_HIST - CONV_STATE + t
                        acc = acc + w_dw_ref[t:t + 1, cols] * u_scr[b, j, off:off + CONV_ROWS, :]
                    conv_scr[b * tb + base:b * tb + base + CONV_ROWS, cols] = acc
                unit += 1
                while next_side < len(side_tasks) and side_after[next_side] <= unit:
                    side_tasks[next_side]()
                    next_side += 1
    while next_side < len(side_tasks):
        side_tasks[next_side]()
        next_side += 1

    key_iota = lax.broadcasted_iota(jnp.int32, (ch, KEY_PAD), 1)
    groups = [(b, c, kh) for b in range(nb) for c in range(n_chunks) for kh in range(N_KV_HEADS)]

    def scores(b, c, kh):
        r0 = b * tb + c * ch
        qp = jnp.concatenate(
            [q_scr[r0:r0 + ch, (kh * PAIRS + i) * 128:(kh * PAIRS + i + 1) * 128] for i in range(PAIRS)], axis=0)
        out = []
        for par in range(2):
            kx = kx_scr[b, kh, par, c * ch:c * ch + KEY_PAD, :]
            out.append(lax.dot_general(qp, kx, (((1,), (1,)), ((), ())), preferred_element_type=f32))
        return out

    def softmax_pv(b, c, kh, s_pair):
        r0 = b * tb + c * ch
        need_mask = (not carry_from_inputs) and (c * ch < WINDOW)
        if need_mask:
            first_pos = seq_tile * tb + (c * ch - WINDOW)
            valid = (key_iota + first_pos) >= 0
        o_pair = []
        sink_terms = []
        for par in range(2):
            ps = []
            for i in range(PAIRS):
                hh = kh * GROUP + 2 * i + par
                sg = s_pair[par][i * ch:(i + 1) * ch] + bias_scr[hh]
                if need_mask:
                    sg = jnp.where(valid, sg, NEG_INF)
                sk = sink_ref[hh] * LOG2E
                m = jnp.maximum(jnp.max(sg, axis=-1, keepdims=True), sk)
                ps.append(jnp.exp2(sg - m).astype(bf16))
                sink_terms.append(jnp.exp2(sk - m))
            pcat = jnp.concatenate(ps, axis=0)
            vx = vx_scr[b, kh, par, c * ch:c * ch + KEY_PAD, :]
            o_pair.append(jnp.dot(pcat, vx, preferred_element_type=f32))
        for i in range(PAIRS):
            rows = slice(i * ch, (i + 1) * ch)
            blk = None
            for par in range(2):
                o = o_pair[par]
                den = o[rows, KV_WIDTH:2 * KV_WIDTH] + sink_terms[par * PAIRS + i]
                term = o[rows, 0:KV_WIDTH] * (1.0 / den)
                blk = term if blk is None else blk + term
            c0 = (kh * PAIRS + i) * 128
            attn_scr[r0:r0 + ch, c0:c0 + 128] = blk

    s_next = scores(*groups[0])
    for r0, b, t0 in row_blocks():
        y = conv_scr[r0:r0 + ROW_BLOCK, :]
        mu = jnp.mean(y, axis=-1, keepdims=True)
        yc = y - mu
        yn = yc * lax.rsqrt(jnp.mean(yc * yc, axis=-1, keepdims=True) + EPS) * ln_g_ref[...] + ln_b_ref[...]
        ya_scr[r0:r0 + ROW_BLOCK, :] = (yn * jax.nn.sigmoid(yn)).astype(bf16)

    n_co_chunks = D_MODEL // 256
    co_after = [((k + 1) * len(groups)) // (n_co_chunks + 1) for k in range(n_co_chunks)]
    next_co = 0
    for gi, grp in enumerate(groups):
        s_cur = s_next
        if gi + 1 < len(groups):
            s_next = scores(*groups[gi + 1])
        softmax_pv(*grp, s_cur)
        while next_co < n_co_chunks and co_after[next_co] <= gi + 1:
            c0 = next_co * 256
            convo_scr[:, c0:c0 + 256] = jnp.dot(ya_scr[...], w_co_ref[:, c0:c0 + 256], preferred_element_type=f32)
            next_co += 1

    for r0, b, t0 in row_blocks():
        rows = slice(r0, r0 + ROW_BLOCK)
        mix = gate_scr[rows, 0:D_MODEL] * attn_scr[rows, :] + gate_scr[rows, D_MODEL:2 * D_MODEL] * convo_scr[rows, :]
        mix_scr[rows, :] = mix.astype(bf16)
    for c0 in range(0, D_MODEL, 256):
        proj_scr[:, c0:c0 + 256] = jnp.dot(mix_scr[...], w_out_ref[:, c0:c0 + 256], preferred_element_type=f32)
    for r0, b, t0 in row_blocks():
        xb = x_ref[b, t0:t0 + ROW_BLOCK, :]
        x1_ref[b, t0:t0 + ROW_BLOCK, :] = xb + _rms(proj_scr[r0:r0 + ROW_BLOCK, :], g_post_ref[...])


def _const_spec(shape):
    zeros = (0,) * len(shape)
    return pl.BlockSpec(shape, lambda i, j: zeros, pipeline_mode=pl.Buffered(1))


def _mixer(x, hist, table, sink, params, *, nb, tb, ch):
    B, S, _ = x.shape
    carry_from_inputs = hist is not None
    nkeys = WINDOW + ch
    assert nkeys <= KEY_PAD
    rel = jnp.arange(nkeys)[None, :] - WINDOW - jnp.arange(ch)[:, None]
    bucket = jnp.pad(_t5_bucket(rel).astype(jnp.int32), ((0, 0), (0, KEY_PAD - nkeys)), constant_values=-1)
    g_pre, w_in, w_dw, b_dw, ln_g, ln_b, w_co, w_out, g_post = params
    key_rows = (tb // ch - 1) * ch + KEY_PAD

    grid = (B // nb, S // tb)
    smem = pl.BlockSpec(memory_space=pltpu.SMEM)
    in_specs = [smem, smem, _const_spec((ch, KEY_PAD)),
                pl.BlockSpec((nb, tb, D_MODEL), lambda i, j: (i, j, 0))]
    args = [table, sink, bucket, x]
    if carry_from_inputs:
        in_specs += [pl.BlockSpec((nb, WINDOW, KV_WIDTH), lambda i, j: (i, 0, 0)),
                     pl.BlockSpec((nb, WINDOW, KV_WIDTH), lambda i, j: (i, 0, 0)),
                     pl.BlockSpec((nb, CONV_STATE, D_MODEL), lambda i, j: (i, 0, 0))]
        args += list(hist)
    weights = [g_pre, w_in, w_dw, b_dw, ln_g, ln_b, w_co, w_out, g_post]
    in_specs += [_const_spec(w.shape) for w in weights]
    args += weights

    out_shape = (jax.ShapeDtypeStruct((B, S, D_MODEL), jnp.float32),
                 jax.ShapeDtypeStruct((B, WINDOW, KV_WIDTH), jnp.float32),
                 jax.ShapeDtypeStruct((B, WINDOW, KV_WIDTH), jnp.float32),
                 jax.ShapeDtypeStruct((B, CONV_STATE, D_MODEL), jnp.float32))
    out_specs = (pl.BlockSpec((nb, tb, D_MODEL), lambda i, j: (i, j, 0)),
                 pl.BlockSpec((nb, WINDOW, KV_WIDTH), lambda i, j: (i, 0, 0)),
                 pl.BlockSpec((nb, WINDOW, KV_WIDTH), lambda i, j: (i, 0, 0)),
                 pl.BlockSpec((nb, CONV_STATE, D_MODEL), lambda i, j: (i, 0, 0)))
    f32, bf16 = jnp.float32, jnp.bfloat16
    scratch = [pltpu.VMEM((N_HEADS, ch, KEY_PAD), f32),
               pltpu.VMEM((nb, N_KV_HEADS, 2, key_rows, KV_WIDTH), bf16),
               pltpu.VMEM((nb, N_KV_HEADS, 2, key_rows, 2 * KV_WIDTH), bf16),
               pltpu.VMEM((nb, N_COL_BLOCKS, CONV_HIST + tb, 128), f32),
               pltpu.VMEM((nb * tb, D_MODEL), f32),
               pltpu.VMEM((nb * tb, D_MODEL), f32),
               pltpu.VMEM((nb * tb, D_MODEL), f32),
               pltpu.VMEM((nb * tb, 2 * D_MODEL), f32),
               pltpu.VMEM((nb * tb, D_MODEL), bf16),
               pltpu.VMEM((nb * tb, D_MODEL), bf16),
               pltpu.VMEM((nb * tb, D_MODEL), bf16),
               pltpu.VMEM((nb * tb, D_MODEL), bf16),
               pltpu.VMEM((nb * tb, D_MODEL), f32)]
    kern = functools.partial(_mixer_kernel, nb=nb, tb=tb, ch=ch, carry_from_inputs=carry_from_inputs)
    return pl.pallas_call(
        kern, grid=grid, in_specs=in_specs, out_specs=out_specs, out_shape=out_shape,
        scratch_shapes=scratch,
        compiler_params=pltpu.CompilerParams(dimension_semantics=("arbitrary", "arbitrary"),
                                             vmem_limit_bytes=VMEM_LIMIT_BYTES),
        name="mixer_sample" if carry_from_inputs else "mixer_prompt",
    )(*args)


def _ffn_kernel(x_ref, g_pre_ref, wg_ref, wu_ref, wd_ref, g_post_ref, y_ref):
    x = x_ref[...]
    h = _rms(x, g_pre_ref[...]).astype(jnp.bfloat16)
    a = jnp.dot(h, wg_ref[...], preferred_element_type=jnp.float32)
    b = jnp.dot(h, wu_ref[...], preferred_element_type=jnp.float32)
    act = a * jax.nn.sigmoid(a) * b
    f = _bf16_dot(act, wd_ref[...])
    y_ref[...] = x + _rms(f, g_post_ref[...])


def _ffn(x, params, *, tm, name):
    n = x.shape[0]
    const = lambda shape: pl.BlockSpec(shape, lambda i: (0,) * len(shape), pipeline_mode=pl.Buffered(1))
    return pl.pallas_call(
        _ffn_kernel, grid=(n // tm,),
        in_specs=[pl.BlockSpec((tm, D_MODEL), lambda i: (i, 0))] + [const(w.shape) for w in params],
        out_specs=pl.BlockSpec((tm, D_MODEL), lambda i: (i, 0)),
        out_shape=jax.ShapeDtypeStruct((n, D_MODEL), jnp.float32),
        compiler_params=pltpu.CompilerParams(dimension_semantics=("arbitrary",),
                                             vmem_limit_bytes=VMEM_LIMIT_BYTES),
        name=name,
    )(x, *params)


def kernel(x_prompt, x_sample, cache_k, cache_v, state_conv, rel_bias_table, attn_sink, g_pre_mix, w_in, w_dw, b_dw, ln_g, ln_b, w_conv_out, w_out, g_post_mix, g_pre_ffn, w_ffn_gate, w_ffn_up, w_ffn_down, g_post_ffn):
    depth = w_in.shape[0]
    assert depth == 1, "single-layer step"
    bf = jnp.bfloat16
    row = lambda a: a.reshape(1, -1)
    l = 0
    mixer_params = (row(g_pre_mix[l]), w_in[l].astype(bf), w_dw[l], row(b_dw[l]), row(ln_g[l]), row(ln_b[l]),
                    w_conv_out[l].astype(bf), w_out[l].astype(bf), row(g_post_mix[l]))
    ffn_params = (row(g_pre_ffn[l]), w_ffn_gate[l].astype(bf), w_ffn_up[l].astype(bf),
                  w_ffn_down[l].astype(bf), row(g_post_ffn[l]))
    sink = attn_sink[l]

    B, S, _ = x_prompt.shape
    xp1, kp, vp, cp = _mixer(x_prompt, None, rel_bias_table, sink, mixer_params, nb=1, tb=256, ch=CHUNK)
    yp = _ffn(xp1.reshape(B * S, D_MODEL), ffn_params, tm=512, name="ffn_prompt").reshape(B, S, D_MODEL)

    Bs, Ss, _ = x_sample.shape
    hist = (cache_k[l].reshape(Bs, WINDOW, KV_WIDTH), cache_v[l].reshape(Bs, WINDOW, KV_WIDTH), state_conv[l])
    xs1, ks, vs, cs = _mixer(x_sample, hist, rel_bias_table, sink, mixer_params, nb=8, tb=Ss, ch=Ss)
    ys = _ffn(xs1.reshape(Bs * Ss, D_MODEL), ffn_params, tm=512, name="ffn_sample").reshape(Bs, Ss, D_MODEL)

    kv5 = lambda a: a.reshape(1, a.shape[0], WINDOW, N_KV_HEADS, HEAD_DIM)
    return (yp, ys, kv5(kp), kv5(vp), cp[None], kv5(ks), kv5(vs), cs[None])
```

```python
import functools
import math

import jax
import jax.numpy as jnp
from jax import lax
from jax.experimental import pallas as pl
from jax.experimental.pallas import tpu as pltpu

D_MODEL = 1024
HEAD_DIM = 64
N_HEADS = 16
N_KV_HEADS = 2
GROUP = N_HEADS // N_KV_HEADS
PAIRS = GROUP // 2
KV_WIDTH = N_KV_HEADS * HEAD_DIM
WINDOW = 128
CHUNK = 64
KEY_PAD = 256
CONV_WIDTH = 31
CONV_STATE = CONV_WIDTH - 1
CONV_HIST = 32
CONV_ROWS = 32
N_COL_BLOCKS = D_MODEL // 128
ROW_BLOCK = 16
D_FF = 2816
N_BUCKETS = 32
MAX_DISTANCE = 128
EPS = 1e-6
NEG_INF = -1e30
LOG2E = math.log2(math.e)
IN_WIDTH = D_MODEL + 2 * KV_WIDTH + 2 * D_MODEL + 2 * D_MODEL

_Q0 = 0
_K0 = D_MODEL
_V0 = _K0 + KV_WIDTH
_UA0 = _V0 + KV_WIDTH
_UB0 = _UA0 + D_MODEL
_GA0 = _UB0 + D_MODEL

VMEM_LIMIT_BYTES = 56 * 1024 * 1024


def _t5_bucket(rel):
    half = N_BUCKETS // 2
    max_exact = half // 2
    ret = jnp.where(rel > 0, half, 0)
    n = jnp.abs(rel)
    nf = jnp.maximum(n, 1).astype(jnp.float32)
    large = max_exact + (jnp.log(nf / max_exact) / math.log(MAX_DISTANCE / max_exact)
                         * (half - max_exact)).astype(jnp.int32)
    large = jnp.minimum(large, half - 1)
    return ret + jnp.where(n < max_exact, n, large)


def _rms(x, g):
    return x * lax.rsqrt(jnp.mean(x * x, axis=-1, keepdims=True) + EPS) * g


def _bf16_dot(a, w):
    return jnp.dot(a.astype(jnp.bfloat16), w, preferred_element_type=jnp.float32)


def _mixer_kernel(*refs, nb, tb, ch, carry_from_inputs):
    it = iter(refs)
    table_ref = next(it)
    sink_ref = next(it)
    bucket_ref = next(it)
    x_ref = next(it)
    if carry_from_inputs:
        ck_ref = next(it)
        cv_ref = next(it)
        cs_ref = next(it)
    g_pre_ref = next(it)
    w_in_ref = next(it)
    w_dw_ref = next(it)
    b_dw_ref = next(it)
    ln_g_ref = next(it)
    ln_b_ref = next(it)
    w_co_ref = next(it)
    w_out_ref = next(it)
    g_post_ref = next(it)
    x1_ref = next(it)
    kwin_ref = next(it)
    vwin_ref = next(it)
    cst_ref = next(it)
    bias_scr = next(it)
    kx_scr = next(it)
    vx_scr = next(it)
    u_scr = next(it)
    attn_scr = next(it)
    conv_scr = next(it)
    convo_scr = next(it)
    gate_scr = next(it)
    h_scr = next(it)
    q_scr = next(it)
    ya_scr = next(it)
    mix_scr = next(it)
    proj_scr = next(it)

    f32, bf16 = jnp.float32, jnp.bfloat16
    n_chunks = tb // ch
    key_rows = kx_scr.shape[3]
    first_step = (pl.program_id(0) == 0) & (pl.program_id(1) == 0)
    seq_tile = pl.program_id(1)
    low_lanes = lax.broadcasted_iota(jnp.int32, (1, KV_WIDTH), 1) < HEAD_DIM

    @pl.when(first_step)
    def _first_step():
        bucket = bucket_ref[...]
        pad_bias = jnp.where(bucket < 0, NEG_INF, 0.0)
        for h in range(N_HEADS):
            bias_scr[h] = pad_bias

        def body(b, carry):
            hit = bucket == b
            for h in range(N_HEADS):
                bias_scr[h] = jnp.where(hit, table_ref[b, h] * LOG2E, bias_scr[h])
            return carry

        lax.fori_loop(0, N_BUCKETS, body, 0)
        kx_scr[...] = jnp.zeros_like(kx_scr)
        vx_scr[:, :, :, :, 0:KV_WIDTH] = jnp.zeros(vx_scr.shape[:4] + (KV_WIDTH,), bf16)
        vx_scr[:, :, :, :, KV_WIDTH:2 * KV_WIDTH] = jnp.ones(vx_scr.shape[:4] + (KV_WIDTH,), bf16)

    def store_kv(b, row0, k, v):
        n = k.shape[0]
        k_sw = pltpu.roll(k, HEAD_DIM, axis=1)
        v_sw = pltpu.roll(v, HEAD_DIM, axis=1)
        zero = jnp.zeros((), f32)
        for kh, (same, swapped) in enumerate(((k, k_sw), (k_sw, k))):
            kx_scr[b, kh, 0, row0:row0 + n, :] = jnp.where(low_lanes, same, zero).astype(bf16)
            kx_scr[b, kh, 1, row0:row0 + n, :] = jnp.where(low_lanes, zero, swapped).astype(bf16)
        for kh, (same, swapped) in enumerate(((v, v_sw), (v_sw, v))):
            vx_scr[b, kh, 0, row0:row0 + n, 0:KV_WIDTH] = jnp.where(low_lanes, same, zero).astype(bf16)
            vx_scr[b, kh, 1, row0:row0 + n, 0:KV_WIDTH] = jnp.where(low_lanes, zero, swapped).astype(bf16)

    if carry_from_inputs:
        for b in range(nb):
            store_kv(b, 0, ck_ref[b], cv_ref[b])
            for j in range(N_COL_BLOCKS):
                u_scr[b, j, 0:8, :] = jnp.zeros((8, 128), f32)
                u_scr[b, j, CONV_HIST - CONV_STATE:CONV_HIST, :] = cs_ref[b, :, j * 128:(j + 1) * 128]
    else:
        @pl.when(seq_tile == 0)
        def _reset_history():
            for b in range(nb):
                for kh in range(N_KV_HEADS):
                    for par in range(2):
                        kx_scr[b, kh, par, 0:WINDOW, :] = jnp.zeros((WINDOW, KV_WIDTH), bf16)
                        vx_scr[b, kh, par, 0:WINDOW, 0:KV_WIDTH] = jnp.zeros((WINDOW, KV_WIDTH), bf16)
                u_scr[b, :, 0:CONV_HIST, :] = jnp.zeros((N_COL_BLOCKS, CONV_HIST, 128), f32)

        @pl.when(seq_tile != 0)
        def _roll_history():
            for b in range(nb):
                for kh in range(N_KV_HEADS):
                    for par in range(2):
                        kx_scr[b, kh, par, 0:WINDOW, :] = kx_scr[b, kh, par, tb:tb + WINDOW, :]
                        vx_scr[b, kh, par, 0:WINDOW, 0:KV_WIDTH] = vx_scr[b, kh, par, tb:tb + WINDOW, 0:KV_WIDTH]
                u_scr[b, :, 0:CONV_HIST, :] = u_scr[b, :, tb:tb + CONV_HIST, :]

    m_rows = nb * tb

    def row_blocks():
        for r0 in range(0, m_rows, ROW_BLOCK):
            b, t0 = divmod(r0, tb)
            yield r0, b, t0

    for r0, b, t0 in row_blocks():
        xb = x_ref[b, t0:t0 + ROW_BLOCK, :]
        h_scr[r0:r0 + ROW_BLOCK, :] = _rms(xb, g_pre_ref[...]).astype(bf16)

    def proj(c0, n):
        return jnp.dot(h_scr[...], w_in_ref[:, c0:c0 + n], preferred_element_type=f32)

    def do_q(i):
        c0 = i * 512
        q_scr[:, c0:c0 + 512] = (proj(_Q0 + c0, 512) * (HEAD_DIM ** -0.5 * LOG2E)).astype(bf16)

    def do_kv():
        kv = proj(_K0, 2 * KV_WIDTH)
        k = kv[:, 0:KV_WIDTH]
        v = kv[:, KV_WIDTH:2 * KV_WIDTH]
        for b in range(nb):
            rows = slice(b * tb, (b + 1) * tb)
            store_kv(b, WINDOW, k[rows], v[rows])
            if tb >= WINDOW:
                kwin_ref[b] = k[b * tb + tb - WINDOW:(b + 1) * tb]
                vwin_ref[b] = v[b * tb + tb - WINDOW:(b + 1) * tb]
            else:
                kwin_ref[b, 0:WINDOW - tb, :] = ck_ref[b, tb:WINDOW, :]
                vwin_ref[b, 0:WINDOW - tb, :] = cv_ref[b, tb:WINDOW, :]
                kwin_ref[b, WINDOW - tb:WINDOW, :] = k[rows]
                vwin_ref[b, WINDOW - tb:WINDOW, :] = v[rows]

    def do_gate(i):
        c0 = i * 256
        gate_scr[:, c0:c0 + 256] = jax.nn.sigmoid(proj(_GA0 + c0, 256))

    side_tasks = [functools.partial(do_q, 0), functools.partial(do_q, 1), do_kv]
    side_tasks += [functools.partial(do_gate, i) for i in range(2 * D_MODEL // 256)]

    def glu_block(j):
        z = proj(_UA0 + j * 256, 256)
        return z[:, 0:128] * jax.nn.sigmoid(z[:, 128:256])

    units_per_block = nb * (tb // CONV_ROWS)
    total_units = N_COL_BLOCKS * units_per_block
    first_side = 2 * units_per_block
    side_after = [first_side + (k * (total_units - first_side)) // len(side_tasks) for k in range(len(side_tasks))]
    next_side = 0
    unit = 0
    u_next = glu_block(0)
    for j in range(N_COL_BLOCKS):
        u_cur = u_next
        cols = slice(j * 128, (j + 1) * 128)
        for b in range(nb):
            u_scr[b, j, CONV_HIST:CONV_HIST + tb, :] = u_cur[b * tb:(b + 1) * tb, :]
            cst_ref[b, :, cols] = u_scr[b, j, CONV_HIST + tb - CONV_STATE:CONV_HIST + tb, :]
        if j + 1 < N_COL_BLOCKS:
            u_next = glu_block(j + 1)
        for b in range(nb):
            for r in range(tb // CONV_ROWS):
                base = r * CONV_ROWS
                acc = jnp.broadcast_to(b_dw_ref[:, cols], (CONV_ROWS, 128))
                for t in range(CONV_WIDTH):
                    off = base + CONV_HIST - CONV_STATE + t
                    acc = acc + w_dw_ref[t:t + 1, cols] * u_scr[b, j, off:off + CONV_ROWS, :]
                conv_scr[b * tb + base:b * tb + base + CONV_ROWS, cols] = acc
                unit += 1
                while next_side < len(side_tasks) and side_after[next_side] <= unit:
                    side_tasks[next_side]()
                    next_side += 1
    while next_side < len(side_tasks):
        side_tasks[next_side]()
        next_side += 1

    key_iota = lax.broadcasted_iota(jnp.int32, (ch, KEY_PAD), 1)
    groups = [(b, c, kh) for b in range(nb) for c in range(n_chunks) for kh in range(N_KV_HEADS)]

    def scores(b, c, kh):
        r0 = b * tb + c * ch
        qp = jnp.concatenate(
            [q_scr[r0:r0 + ch, (kh * PAIRS + i) * 128:(kh * PAIRS + i + 1) * 128] for i in range(PAIRS)], axis=0)
        out = []
        for par in range(2):
            kx = kx_scr[b, kh, par, c * ch:c * ch + KEY_PAD, :]
            out.append(lax.dot_general(qp, kx, (((1,), (1,)), ((), ())), preferred_element_type=f32))
        return out

    def softmax_pv(b, c, kh, s_pair):
        r0 = b * tb + c * ch
        need_mask = (not carry_from_inputs) and (c * ch < WINDOW)
        if need_mask:
            first_pos = seq_tile * tb + (c * ch - WINDOW)
            valid = (key_iota + first_pos) >= 0
        o_pair = []
        sink_terms = []
        for par in range(2):
            ps = []
            for i in range(PAIRS):
                hh = kh * GROUP + 2 * i + par
                sg = s_pair[par][i * ch:(i + 1) * ch] + bias_scr[hh]
                if need_mask:
                    sg = jnp.where(valid, sg, NEG_INF)
                sk = sink_ref[hh] * LOG2E
                m = jnp.maximum(jnp.max(sg, axis=-1, keepdims=True), sk)
                ps.append(jnp.exp2(sg - m).astype(bf16))
                sink_terms.append(jnp.exp2(sk - m))
            pcat = jnp.concatenate(ps, axis=0)
            vx = vx_scr[b, kh, par, c * ch:c * ch + KEY_PAD, :]
            o_pair.append(jnp.dot(pcat, vx, preferred_element_type=f32))
        for i in range(PAIRS):
            rows = slice(i * ch, (i + 1) * ch)
            blk = None
            for par in range(2):
                o = o_pair[par]
                den = o[rows, KV_WIDTH:2 * KV_WIDTH] + sink_terms[par * PAIRS + i]
                term = o[rows, 0:KV_WIDTH] * (1.0 / den)
                blk = term if blk is None else blk + term
            c0 = (kh * PAIRS + i) * 128
            attn_scr[r0:r0 + ch, c0:c0 + 128] = blk

    s_next = scores(*groups[0])
    for r0, b, t0 in row_blocks():
        y = conv_scr[r0:r0 + ROW_BLOCK, :]
        mu = jnp.mean(y, axis=-1, keepdims=True)
        yc = y - mu
        yn = yc * lax.rsqrt(jnp.mean(yc * yc, axis=-1, keepdims=True) + EPS) * ln_g_ref[...] + ln_b_ref[...]
        ya_scr[r0:r0 + ROW_BLOCK, :] = (yn * jax.nn.sigmoid(yn)).astype(bf16)

    n_co_chunks = D_MODEL // 256
    co_after = [((k + 1) * len(groups)) // (n_co_chunks + 1) for k in range(n_co_chunks)]
    next_co = 0
    for gi, grp in enumerate(groups):
        s_cur = s_next
        if gi + 1 < len(groups):
            s_next = scores(*groups[gi + 1])
        softmax_pv(*grp, s_cur)
        while next_co < n_co_chunks and co_after[next_co] <= gi + 1:
            c0 = next_co * 256
            convo_scr[:, c0:c0 + 256] = jnp.dot(ya_scr[...], w_co_ref[:, c0:c0 + 256], preferred_element_type=f32)
            next_co += 1

    for r0, b, t0 in row_blocks():
        rows = slice(r0, r0 + ROW_BLOCK)
        mix = gate_scr[rows, 0:D_MODEL] * attn_scr[rows, :] + gate_scr[rows, D_MODEL:2 * D_MODEL] * convo_scr[rows, :]
        mix_scr[rows, :] = mix.astype(bf16)
    for c0 in range(0, D_MODEL, 256):
        proj_scr[:, c0:c0 + 256] = jnp.dot(mix_scr[...], w_out_ref[:, c0:c0 + 256], preferred_element_type=f32)
    for r0, b, t0 in row_blocks():
        xb = x_ref[b, t0:t0 + ROW_BLOCK, :]
        x1_ref[b, t0:t0 + ROW_BLOCK, :] = xb + _rms(proj_scr[r0:r0 + ROW_BLOCK, :], g_post_ref[...])


def _const_spec(shape):
    zeros = (0,) * len(shape)
    return pl.BlockSpec(shape, lambda i, j: zeros, pipeline_mode=pl.Buffered(1))


def _mixer(x, hist, table, sink, params, *, nb, tb, ch):
    B, S, _ = x.shape
    carry_from_inputs = hist is not None
    nkeys = WINDOW + ch
    assert nkeys <= KEY_PAD
    rel = jnp.arange(nkeys)[None, :] - WINDOW - jnp.arange(ch)[:, None]
    bucket = jnp.pad(_t5_bucket(rel).astype(jnp.int32), ((0, 0), (0, KEY_PAD - nkeys)), constant_values=-1)
    g_pre, w_in, w_dw, b_dw, ln_g, ln_b, w_co, w_out, g_post = params
    key_rows = (tb // ch - 1) * ch + KEY_PAD

    grid = (B // nb, S // tb)
    smem = pl.BlockSpec(memory_space=pltpu.SMEM)
    in_specs = [smem, smem, _const_spec((ch, KEY_PAD)),
                pl.BlockSpec((nb, tb, D_MODEL), lambda i, j: (i, j, 0))]
    args = [table, sink, bucket, x]
    if carry_from_inputs:
        in_specs += [pl.BlockSpec((nb, WINDOW, KV_WIDTH), lambda i, j: (i, 0, 0)),
                     pl.BlockSpec((nb, WINDOW, KV_WIDTH), lambda i, j: (i, 0, 0)),
                     pl.BlockSpec((nb, CONV_STATE, D_MODEL), lambda i, j: (i, 0, 0))]
        args += list(hist)
    weights = [g_pre, w_in, w_dw, b_dw, ln_g, ln_b, w_co, w_out, g_post]
    in_specs += [_const_spec(w.shape) for w in weights]
    args += weights

    out_shape = (jax.ShapeDtypeStruct((B, S, D_MODEL), jnp.float32),
                 jax.ShapeDtypeStruct((B, WINDOW, KV_WIDTH), jnp.float32),
                 jax.ShapeDtypeStruct((B, WINDOW, KV_WIDTH), jnp.float32),
                 jax.ShapeDtypeStruct((B, CONV_STATE, D_MODEL), jnp.float32))
    out_specs = (pl.BlockSpec((nb, tb, D_MODEL), lambda i, j: (i, j, 0)),
                 pl.BlockSpec((nb, WINDOW, KV_WIDTH), lambda i, j: (i, 0, 0)),
                 pl.BlockSpec((nb, WINDOW, KV_WIDTH), lambda i, j: (i, 0, 0)),
                 pl.BlockSpec((nb, CONV_STATE, D_MODEL), lambda i, j: (i, 0, 0)))
    f32, bf16 = jnp.float32, jnp.bfloat16
    scratch = [pltpu.VMEM((N_HEADS, ch, KEY_PAD), f32),
               pltpu.VMEM((nb, N_KV_HEADS, 2, key_rows, KV_WIDTH), bf16),
               pltpu.VMEM((nb, N_KV_HEADS, 2, key_rows, 2 * KV_WIDTH), bf16),
               pltpu.VMEM((nb, N_COL_BLOCKS, CONV_HIST + tb, 128), f32),
               pltpu.VMEM((nb * tb, D_MODEL), f32),
               pltpu.VMEM((nb * tb, D_MODEL), f32),
               pltpu.VMEM((nb * tb, D_MODEL), f32),
               pltpu.VMEM((nb * tb, 2 * D_MODEL), f32),
               pltpu.VMEM((nb * tb, D_MODEL), bf16),
               pltpu.VMEM((nb * tb, D_MODEL), bf16),
               pltpu.VMEM((nb * tb, D_MODEL), bf16),
               pltpu.VMEM((nb * tb, D_MODEL), bf16),
               pltpu.VMEM((nb * tb, D_MODEL), f32)]
    kern = functools.partial(_mixer_kernel, nb=nb, tb=tb, ch=ch, carry_from_inputs=carry_from_inputs)
    return pl.pallas_call(
        kern, grid=grid, in_specs=in_specs, out_specs=out_specs, out_shape=out_shape,
        scratch_shapes=scratch,
        compiler_params=pltpu.CompilerParams(dimension_semantics=("arbitrary", "arbitrary"),
                                             vmem_limit_bytes=VMEM_LIMIT_BYTES),
        name="mixer_sample" if carry_from_inputs else "mixer_prompt",
    )(*args)


def _ffn_kernel(x_ref, g_pre_ref, wg_ref, wu_ref, wd_ref, g_post_ref, y_ref):
    x = x_ref[...]
    h = _rms(x, g_pre_ref[...]).astype(jnp.bfloat16)
    a = jnp.dot(h, wg_ref[...], preferred_element_type=jnp.float32)
    b = jnp.dot(h, wu_ref[...], preferred_element_type=jnp.float32)
    act = a * jax.nn.sigmoid(a) * b
    f = _bf16_dot(act, wd_ref[...])
    y_ref[...] = x + _rms(f, g_post_ref[...])


def _ffn(x, params, *, tm, name):
    n = x.shape[0]
    const = lambda shape: pl.BlockSpec(shape, lambda i: (0,) * len(shape), pipeline_mode=pl.Buffered(1))
    return pl.pallas_call(
        _ffn_kernel, grid=(n // tm,),
        in_specs=[pl.BlockSpec((tm, D_MODEL), lambda i: (i, 0))] + [const(w.shape) for w in params],
        out_specs=pl.BlockSpec((tm, D_MODEL), lambda i: (i, 0)),
        out_shape=jax.ShapeDtypeStruct((n, D_MODEL), jnp.float32),
        compiler_params=pltpu.CompilerParams(dimension_semantics=("arbitrary",),
                                             vmem_limit_bytes=VMEM_LIMIT_BYTES),
        name=name,
    )(x, *params)


def kernel(x_prompt, x_sample, cache_k, cache_v, state_conv, rel_bias_table, attn_sink, g_pre_mix, w_in, w_dw, b_dw, ln_g, ln_b, w_conv_out, w_out, g_post_mix, g_pre_ffn, w_ffn_gate, w_ffn_up, w_ffn_down, g_post_ffn):
    depth = w_in.shape[0]
    assert depth == 1, "single-layer step"
    bf = jnp.bfloat16
    row = lambda a: a.reshape(1, -1)
    l = 0
    w_ua = w_in[l][:, _UA0:_UB0].reshape(D_MODEL, N_COL_BLOCKS, 1, 128)
    w_ub = w_in[l][:, _UB0:_GA0].reshape(D_MODEL, N_COL_BLOCKS, 1, 128)
    w_glu = jnp.concatenate([w_ua, w_ub], axis=2).reshape(D_MODEL, 2 * D_MODEL)
    w_in_k = jnp.concatenate([w_in[l][:, :_UA0], w_glu, w_in[l][:, _GA0:]], axis=1).astype(bf)
    mixer_params = (row(g_pre_mix[l]), w_in_k, w_dw[l], row(b_dw[l]), row(ln_g[l]), row(ln_b[l]),
                    w_conv_out[l].astype(bf), w_out[l].astype(bf), row(g_post_mix[l]))
    ffn_params = (row(g_pre_ffn[l]), w_ffn_gate[l].astype(bf), w_ffn_up[l].astype(bf),
                  w_ffn_down[l].astype(bf), row(g_post_ffn[l]))
    sink = attn_sink[l]

    B, S, _ = x_prompt.shape
    xp1, kp, vp, cp = _mixer(x_prompt, None, rel_bias_table, sink, mixer_params, nb=1, tb=512, ch=CHUNK)
    yp = _ffn(xp1.reshape(B * S, D_MODEL), ffn_params, tm=512, name="ffn_prompt").reshape(B, S, D_MODEL)

    Bs, Ss, _ = x_sample.shape
    hist = (cache_k[l].reshape(Bs, WINDOW, KV_WIDTH), cache_v[l].reshape(Bs, WINDOW, KV_WIDTH), state_conv[l])
    xs1, ks, vs, cs = _mixer(x_sample, hist, rel_bias_table, sink, mixer_params, nb=8, tb=Ss, ch=Ss)
    ys = _ffn(xs1.reshape(Bs * Ss, D_MODEL), ffn_params, tm=512, name="ffn_sample").reshape(Bs, Ss, D_MODEL)

    kv5 = lambda a: a.reshape(1, a.shape[0], WINDOW, N_KV_HEADS, HEAD_DIM)
    return (yp, ys, kv5(kp), kv5(vp), cp[None], kv5(ks), kv5(vs), cs[None])
```

```python
import functools
import math

import jax
import jax.numpy as jnp
from jax import lax
from jax.experimental import pallas as pl
from jax.experimental.pallas import tpu as pltpu

D_MODEL = 1024
HEAD_DIM = 64
N_HEADS = 16
N_KV_HEADS = 2
GROUP = N_HEADS // N_KV_HEADS
PAIRS = GROUP // 2
KV_WIDTH = N_KV_HEADS * HEAD_DIM
WINDOW = 128
CHUNK = 64
KEY_PAD = 256
CONV_WIDTH = 31
CONV_STATE = CONV_WIDTH - 1
CONV_HIST = 32
CONV_ROWS = 32
N_COL_BLOCKS = D_MODEL // 128
ROW_BLOCK = 16
FFN_HALVES = 2
D_FF = 2816
N_BUCKETS = 32
MAX_DISTANCE = 128
EPS = 1e-6
NEG_INF = -1e30
LOG2E = math.log2(math.e)
IN_WIDTH = D_MODEL + 2 * KV_WIDTH + 2 * D_MODEL + 2 * D_MODEL

_Q0 = 0
_K0 = D_MODEL
_V0 = _K0 + KV_WIDTH
_UA0 = _V0 + KV_WIDTH
_UB0 = _UA0 + D_MODEL
_GA0 = _UB0 + D_MODEL

VMEM_LIMIT_BYTES = 56 * 1024 * 1024


def _t5_bucket(rel):
    half = N_BUCKETS // 2
    max_exact = half // 2
    ret = jnp.where(rel > 0, half, 0)
    n = jnp.abs(rel)
    nf = jnp.maximum(n, 1).astype(jnp.float32)
    large = max_exact + (jnp.log(nf / max_exact) / math.log(MAX_DISTANCE / max_exact)
                         * (half - max_exact)).astype(jnp.int32)
    large = jnp.minimum(large, half - 1)
    return ret + jnp.where(n < max_exact, n, large)


def _rms(x, g):
    return x * lax.rsqrt(jnp.mean(x * x, axis=-1, keepdims=True) + EPS) * g


def _bf16_dot(a, w):
    return jnp.dot(a.astype(jnp.bfloat16), w, preferred_element_type=jnp.float32)


def _mixer_kernel(*refs, nb, tb, ch, carry_from_inputs):
    it = iter(refs)
    table_ref = next(it)
    sink_ref = next(it)
    bucket_ref = next(it)
    x_ref = next(it)
    if carry_from_inputs:
        ck_ref = next(it)
        cv_ref = next(it)
        cs_ref = next(it)
    g_pre_ref = next(it)
    w_in_ref = next(it)
    w_dw_ref = next(it)
    b_dw_ref = next(it)
    ln_g_ref = next(it)
    ln_b_ref = next(it)
    w_co_ref = next(it)
    w_out_ref = next(it)
    g_post_ref = next(it)
    x1_ref = next(it)
    kwin_ref = next(it)
    vwin_ref = next(it)
    cst_ref = next(it)
    bias_scr = next(it)
    kx_scr = next(it)
    vx_scr = next(it)
    u_scr = next(it)
    attn_scr = next(it)
    conv_scr = next(it)
    convo_scr = next(it)
    gate_scr = next(it)
    h_scr = next(it)
    q_scr = next(it)
    ya_scr = next(it)
    mix_scr = next(it)
    proj_scr = next(it)

    f32, bf16 = jnp.float32, jnp.bfloat16
    n_chunks = tb // ch
    key_rows = kx_scr.shape[3]
    first_step = (pl.program_id(0) == 0) & (pl.program_id(1) == 0)
    seq_tile = pl.program_id(1)
    low_lanes = lax.broadcasted_iota(jnp.int32, (1, KV_WIDTH), 1) < HEAD_DIM

    @pl.when(first_step)
    def _first_step():
        bucket = bucket_ref[...]
        pad_bias = jnp.where(bucket < 0, NEG_INF, 0.0)
        for h in range(N_HEADS):
            bias_scr[h] = pad_bias

        def body(b, carry):
            hit = bucket == b
            for h in range(N_HEADS):
                bias_scr[h] = jnp.where(hit, table_ref[b, h] * LOG2E, bias_scr[h])
            return carry

        lax.fori_loop(0, N_BUCKETS, body, 0)
        kx_scr[...] = jnp.zeros_like(kx_scr)
        vx_scr[:, :, :, :, 0:KV_WIDTH] = jnp.zeros(vx_scr.shape[:4] + (KV_WIDTH,), bf16)
        vx_scr[:, :, :, :, KV_WIDTH:2 * KV_WIDTH] = jnp.ones(vx_scr.shape[:4] + (KV_WIDTH,), bf16)

    def store_kv(b, row0, k, v):
        n = k.shape[0]
        k_sw = pltpu.roll(k, HEAD_DIM, axis=1)
        v_sw = pltpu.roll(v, HEAD_DIM, axis=1)
        zero = jnp.zeros((), f32)
        for kh, (same, swapped) in enumerate(((k, k_sw), (k_sw, k))):
            kx_scr[b, kh, 0, row0:row0 + n, :] = jnp.where(low_lanes, same, zero).astype(bf16)
            kx_scr[b, kh, 1, row0:row0 + n, :] = jnp.where(low_lanes, zero, swapped).astype(bf16)
        for kh, (same, swapped) in enumerate(((v, v_sw), (v_sw, v))):
            vx_scr[b, kh, 0, row0:row0 + n, 0:KV_WIDTH] = jnp.where(low_lanes, same, zero).astype(bf16)
            vx_scr[b, kh, 1, row0:row0 + n, 0:KV_WIDTH] = jnp.where(low_lanes, zero, swapped).astype(bf16)

    if carry_from_inputs:
        for b in range(nb):
            store_kv(b, 0, ck_ref[b], cv_ref[b])
            for j in range(N_COL_BLOCKS):
                u_scr[b, j, 0:8, :] = jnp.zeros((8, 128), f32)
                u_scr[b, j, CONV_HIST - CONV_STATE:CONV_HIST, :] = cs_ref[b, :, j * 128:(j + 1) * 128]
    else:
        @pl.when(seq_tile == 0)
        def _reset_history():
            for b in range(nb):
                for kh in range(N_KV_HEADS):
                    for par in range(2):
                        kx_scr[b, kh, par, 0:WINDOW, :] = jnp.zeros((WINDOW, KV_WIDTH), bf16)
                        vx_scr[b, kh, par, 0:WINDOW, 0:KV_WIDTH] = jnp.zeros((WINDOW, KV_WIDTH), bf16)
                u_scr[b, :, 0:CONV_HIST, :] = jnp.zeros((N_COL_BLOCKS, CONV_HIST, 128), f32)

        @pl.when(seq_tile != 0)
        def _roll_history():
            for b in range(nb):
                for kh in range(N_KV_HEADS):
                    for par in range(2):
                        kx_scr[b, kh, par, 0:WINDOW, :] = kx_scr[b, kh, par, tb:tb + WINDOW, :]
                        vx_scr[b, kh, par, 0:WINDOW, 0:KV_WIDTH] = vx_scr[b, kh, par, tb:tb + WINDOW, 0:KV_WIDTH]
                u_scr[b, :, 0:CONV_HIST, :] = u_scr[b, :, tb:tb + CONV_HIST, :]

    m_rows = nb * tb

    def row_blocks():
        for r0 in range(0, m_rows, ROW_BLOCK):
            b, t0 = divmod(r0, tb)
            yield r0, b, t0

    for r0, b, t0 in row_blocks():
        xb = x_ref[b, t0:t0 + ROW_BLOCK, :]
        h_scr[r0:r0 + ROW_BLOCK, :] = _rms(xb, g_pre_ref[...]).astype(bf16)

    def proj(c0, n):
        return jnp.dot(h_scr[...], w_in_ref[:, c0:c0 + n], preferred_element_type=f32)

    def do_q(i):
        c0 = i * 512
        q_scr[:, c0:c0 + 512] = (proj(_Q0 + c0, 512) * (HEAD_DIM ** -0.5 * LOG2E)).astype(bf16)

    def do_kv():
        kv = proj(_K0, 2 * KV_WIDTH)
        k = kv[:, 0:KV_WIDTH]
        v = kv[:, KV_WIDTH:2 * KV_WIDTH]
        for b in range(nb):
            rows = slice(b * tb, (b + 1) * tb)
            store_kv(b, WINDOW, k[rows], v[rows])
            if tb >= WINDOW:
                kwin_ref[b] = k[b * tb + tb - WINDOW:(b + 1) * tb]
                vwin_ref[b] = v[b * tb + tb - WINDOW:(b + 1) * tb]
            else:
                kwin_ref[b, 0:WINDOW - tb, :] = ck_ref[b, tb:WINDOW, :]
                vwin_ref[b, 0:WINDOW - tb, :] = cv_ref[b, tb:WINDOW, :]
                kwin_ref[b, WINDOW - tb:WINDOW, :] = k[rows]
                vwin_ref[b, WINDOW - tb:WINDOW, :] = v[rows]

    def do_gate(i):
        c0 = i * 256
        gate_scr[:, c0:c0 + 256] = jax.nn.sigmoid(proj(_GA0 + c0, 256))

    side_tasks = [functools.partial(do_q, 0), functools.partial(do_q, 1), do_kv]
    side_tasks += [functools.partial(do_gate, i) for i in range(2 * D_MODEL // 256)]

    def glu_block(j):
        z = proj(_UA0 + j * 256, 256)
        return z[:, 0:128] * jax.nn.sigmoid(z[:, 128:256])

    units_per_block = nb * (tb // CONV_ROWS)
    total_units = N_COL_BLOCKS * units_per_block
    first_side = 2 * units_per_block
    side_after = [first_side + (k * (total_units - first_side)) // len(side_tasks) for k in range(len(side_tasks))]
    next_side = 0
    unit = 0
    u_next = glu_block(0)
    for j in range(N_COL_BLOCKS):
        u_cur = u_next
        cols = slice(j * 128, (j + 1) * 128)
        for b in range(nb):
            u_scr[b, j, CONV_HIST:CONV_HIST + tb, :] = u_cur[b * tb:(b + 1) * tb, :]
            cst_ref[b, :, cols] = u_scr[b, j, CONV_HIST + tb - CONV_STATE:CONV_HIST + tb, :]
        if j + 1 < N_COL_BLOCKS:
            u_next = glu_block(j + 1)
        for b in range(nb):
            for r in range(tb // CONV_ROWS):
                base = r * CONV_ROWS
                acc = jnp.broadcast_to(b_dw_ref[:, cols], (CONV_ROWS, 128))
                for t in range(CONV_WIDTH):
                    off = base + CONV_HIST - CONV_STATE + t
                    acc = acc + w_dw_ref[t:t + 1, cols] * u_scr[b, j, off:off + CONV_ROWS, :]
                conv_scr[b * tb + base:b * tb + base + CONV_ROWS, cols] = acc
                unit += 1
                while next_side < len(side_tasks) and side_after[next_side] <= unit:
                    side_tasks[next_side]()
                    next_side += 1
    while next_side < len(side_tasks):
        side_tasks[next_side]()
        next_side += 1

    key_iota = lax.broadcasted_iota(jnp.int32, (ch, KEY_PAD), 1)
    groups = [(b, c, kh) for b in range(nb) for c in range(n_chunks) for kh in range(N_KV_HEADS)]

    def scores(b, c, kh):
        r0 = b * tb + c * ch
        qp = jnp.concatenate(
            [q_scr[r0:r0 + ch, (kh * PAIRS + i) * 128:(kh * PAIRS + i + 1) * 128] for i in range(PAIRS)], axis=0)
        out = []
        for par in range(2):
            kx = kx_scr[b, kh, par, c * ch:c * ch + KEY_PAD, :]
            out.append(lax.dot_general(qp, kx, (((1,), (1,)), ((), ())), preferred_element_type=f32))
        return out

    def softmax_pv(b, c, kh, s_pair):
        r0 = b * tb + c * ch
        need_mask = (not carry_from_inputs) and (c * ch < WINDOW)
        if need_mask:
            first_pos = seq_tile * tb + (c * ch - WINDOW)
            valid = (key_iota + first_pos) >= 0
        o_pair = []
        sink_terms = []
        for par in range(2):
            ps = []
            for i in range(PAIRS):
                hh = kh * GROUP + 2 * i + par
                sg = s_pair[par][i * ch:(i + 1) * ch] + bias_scr[hh]
                if need_mask:
                    sg = jnp.where(valid, sg, NEG_INF)
                sk = sink_ref[hh] * LOG2E
                m = jnp.maximum(jnp.max(sg, axis=-1, keepdims=True), sk)
                ps.append(jnp.exp2(sg - m).astype(bf16))
                sink_terms.append(jnp.exp2(sk - m))
            pcat = jnp.concatenate(ps, axis=0)
            vx = vx_scr[b, kh, par, c * ch:c * ch + KEY_PAD, :]
            o_pair.append(jnp.dot(pcat, vx, preferred_element_type=f32))
        for i in range(PAIRS):
            rows = slice(i * ch, (i + 1) * ch)
            blk = None
            for par in range(2):
                o = o_pair[par]
                den = o[rows, KV_WIDTH:2 * KV_WIDTH] + sink_terms[par * PAIRS + i]
                term = o[rows, 0:KV_WIDTH] * (1.0 / den)
                blk = term if blk is None else blk + term
            c0 = (kh * PAIRS + i) * 128
            attn_scr[r0:r0 + ch, c0:c0 + 128] = blk

    s_next = scores(*groups[0])
    for r0, b, t0 in row_blocks():
        y = conv_scr[r0:r0 + ROW_BLOCK, :]
        mu = jnp.mean(y, axis=-1, keepdims=True)
        yc = y - mu
        yn = yc * lax.rsqrt(jnp.mean(yc * yc, axis=-1, keepdims=True) + EPS) * ln_g_ref[...] + ln_b_ref[...]
        ya_scr[r0:r0 + ROW_BLOCK, :] = (yn * jax.nn.sigmoid(yn)).astype(bf16)

    n_co_chunks = D_MODEL // 256
    co_after = [((k + 1) * len(groups)) // (n_co_chunks + 1) for k in range(n_co_chunks)]
    next_co = 0
    for gi, grp in enumerate(groups):
        s_cur = s_next
        if gi + 1 < len(groups):
            s_next = scores(*groups[gi + 1])
        softmax_pv(*grp, s_cur)
        while next_co < n_co_chunks and co_after[next_co] <= gi + 1:
            c0 = next_co * 256
            convo_scr[:, c0:c0 + 256] = jnp.dot(ya_scr[...], w_co_ref[:, c0:c0 + 256], preferred_element_type=f32)
            next_co += 1

    for r0, b, t0 in row_blocks():
        rows = slice(r0, r0 + ROW_BLOCK)
        mix = gate_scr[rows, 0:D_MODEL] * attn_scr[rows, :] + gate_scr[rows, D_MODEL:2 * D_MODEL] * convo_scr[rows, :]
        mix_scr[rows, :] = mix.astype(bf16)
    for c0 in range(0, D_MODEL, 256):
        proj_scr[:, c0:c0 + 256] = jnp.dot(mix_scr[...], w_out_ref[:, c0:c0 + 256], preferred_element_type=f32)
    for r0, b, t0 in row_blocks():
        xb = x_ref[b, t0:t0 + ROW_BLOCK, :]
        x1_ref[b, t0:t0 + ROW_BLOCK, :] = xb + _rms(proj_scr[r0:r0 + ROW_BLOCK, :], g_post_ref[...])


def _const_spec(shape):
    zeros = (0,) * len(shape)
    return pl.BlockSpec(shape, lambda i, j: zeros, pipeline_mode=pl.Buffered(1))


def _mixer(x, hist, table, sink, params, *, nb, tb, ch):
    B, S, _ = x.shape
    carry_from_inputs = hist is not None
    nkeys = WINDOW + ch
    assert nkeys <= KEY_PAD
    rel = jnp.arange(nkeys)[None, :] - WINDOW - jnp.arange(ch)[:, None]
    bucket = jnp.pad(_t5_bucket(rel).astype(jnp.int32), ((0, 0), (0, KEY_PAD - nkeys)), constant_values=-1)
    g_pre, w_in, w_dw, b_dw, ln_g, ln_b, w_co, w_out, g_post = params
    key_rows = (tb // ch - 1) * ch + KEY_PAD

    grid = (B // nb, S // tb)
    smem = pl.BlockSpec(memory_space=pltpu.SMEM)
    in_specs = [smem, smem, _const_spec((ch, KEY_PAD)),
                pl.BlockSpec((nb, tb, D_MODEL), lambda i, j: (i, j, 0))]
    args = [table, sink, bucket, x]
    if carry_from_inputs:
        in_specs += [pl.BlockSpec((nb, WINDOW, KV_WIDTH), lambda i, j: (i, 0, 0)),
                     pl.BlockSpec((nb, WINDOW, KV_WIDTH), lambda i, j: (i, 0, 0)),
                     pl.BlockSpec((nb, CONV_STATE, D_MODEL), lambda i, j: (i, 0, 0))]
        args += list(hist)
    weights = [g_pre, w_in, w_dw, b_dw, ln_g, ln_b, w_co, w_out, g_post]
    in_specs += [_const_spec(w.shape) for w in weights]
    args += weights

    out_shape = (jax.ShapeDtypeStruct((B, S, D_MODEL), jnp.float32),
                 jax.ShapeDtypeStruct((B, WINDOW, KV_WIDTH), jnp.float32),
                 jax.ShapeDtypeStruct((B, WINDOW, KV_WIDTH), jnp.float32),
                 jax.ShapeDtypeStruct((B, CONV_STATE, D_MODEL), jnp.float32))
    out_specs = (pl.BlockSpec((nb, tb, D_MODEL), lambda i, j: (i, j, 0)),
                 pl.BlockSpec((nb, WINDOW, KV_WIDTH), lambda i, j: (i, 0, 0)),
                 pl.BlockSpec((nb, WINDOW, KV_WIDTH), lambda i, j: (i, 0, 0)),
                 pl.BlockSpec((nb, CONV_STATE, D_MODEL), lambda i, j: (i, 0, 0)))
    f32, bf16 = jnp.float32, jnp.bfloat16
    scratch = [pltpu.VMEM((N_HEADS, ch, KEY_PAD), f32),
               pltpu.VMEM((nb, N_KV_HEADS, 2, key_rows, KV_WIDTH), bf16),
               pltpu.VMEM((nb, N_KV_HEADS, 2, key_rows, 2 * KV_WIDTH), bf16),
               pltpu.VMEM((nb, N_COL_BLOCKS, CONV_HIST + tb, 128), f32),
               pltpu.VMEM((nb * tb, D_MODEL), f32),
               pltpu.VMEM((nb * tb, D_MODEL), f32),
               pltpu.VMEM((nb * tb, D_MODEL), f32),
               pltpu.VMEM((nb * tb, 2 * D_MODEL), f32),
               pltpu.VMEM((nb * tb, D_MODEL), bf16),
               pltpu.VMEM((nb * tb, D_MODEL), bf16),
               pltpu.VMEM((nb * tb, D_MODEL), bf16),
               pltpu.VMEM((nb * tb, D_MODEL), bf16),
               pltpu.VMEM((nb * tb, D_MODEL), f32)]
    kern = functools.partial(_mixer_kernel, nb=nb, tb=tb, ch=ch, carry_from_inputs=carry_from_inputs)
    return pl.pallas_call(
        kern, grid=grid, in_specs=in_specs, out_specs=out_specs, out_shape=out_shape,
        scratch_shapes=scratch,
        compiler_params=pltpu.CompilerParams(dimension_semantics=("arbitrary", "arbitrary"),
                                             vmem_limit_bytes=VMEM_LIMIT_BYTES),
        name="mixer_sample" if carry_from_inputs else "mixer_prompt",
    )(*args)


def _ffn_kernel(x_ref, g_pre_ref, wg_ref, wu_ref, wd_ref, g_post_ref, y_ref):
    tm = x_ref.shape[0]
    hm = tm // FFN_HALVES
    hs = [_rms(x_ref[i * hm:(i + 1) * hm, :], g_pre_ref[...]).astype(jnp.bfloat16) for i in range(FFN_HALVES)]
    for i in range(FFN_HALVES):
        rows = slice(i * hm, (i + 1) * hm)
        a = jnp.dot(hs[i], wg_ref[...], preferred_element_type=jnp.float32)
        b = jnp.dot(hs[i], wu_ref[...], preferred_element_type=jnp.float32)
        act = a * jax.nn.sigmoid(a) * b
        f = _bf16_dot(act, wd_ref[...])
        y_ref[rows, :] = x_ref[rows, :] + _rms(f, g_post_ref[...])


def _ffn(x, params, *, tm, name):
    n = x.shape[0]
    const = lambda shape: pl.BlockSpec(shape, lambda i: (0,) * len(shape), pipeline_mode=pl.Buffered(1))
    return pl.pallas_call(
        _ffn_kernel, grid=(n // tm,),
        in_specs=[pl.BlockSpec((tm, D_MODEL), lambda i: (i, 0))] + [const(w.shape) for w in params],
        out_specs=pl.BlockSpec((tm, D_MODEL), lambda i: (i, 0)),
        out_shape=jax.ShapeDtypeStruct((n, D_MODEL), jnp.float32),
        compiler_params=pltpu.CompilerParams(dimension_semantics=("arbitrary",),
                                             vmem_limit_bytes=VMEM_LIMIT_BYTES),
        name=name,
    )(x, *params)


def kernel(x_prompt, x_sample, cache_k, cache_v, state_conv, rel_bias_table, attn_sink, g_pre_mix, w_in, w_dw, b_dw, ln_g, ln_b, w_conv_out, w_out, g_post_mix, g_pre_ffn, w_ffn_gate, w_ffn_up, w_ffn_down, g_post_ffn):
    depth = w_in.shape[0]
    assert depth == 1, "single-layer step"
    bf = jnp.bfloat16
    row = lambda a: a.reshape(1, -1)
    l = 0
    w_ua = w_in[l][:, _UA0:_UB0].reshape(D_MODEL, N_COL_BLOCKS, 1, 128)
    w_ub = w_in[l][:, _UB0:_GA0].reshape(D_MODEL, N_COL_BLOCKS, 1, 128)
    w_glu = jnp.concatenate([w_ua, w_ub], axis=2).reshape(D_MODEL, 2 * D_MODEL)
    w_in_k = jnp.concatenate([w_in[l][:, :_UA0], w_glu, w_in[l][:, _GA0:]], axis=1).astype(bf)
    mixer_params = (row(g_pre_mix[l]), w_in_k, w_dw[l], row(b_dw[l]), row(ln_g[l]), row(ln_b[l]),
                    w_conv_out[l].astype(bf), w_out[l].astype(bf), row(g_post_mix[l]))
    ffn_params = (row(g_pre_ffn[l]), w_ffn_gate[l].astype(bf), w_ffn_up[l].astype(bf),
                  w_ffn_down[l].astype(bf), row(g_post_ffn[l]))
    sink = attn_sink[l]

    B, S, _ = x_prompt.shape
    xp1, kp, vp, cp = _mixer(x_prompt, None, rel_bias_table, sink, mixer_params, nb=1, tb=512, ch=CHUNK)
    yp = _ffn(xp1.reshape(B * S, D_MODEL), ffn_params, tm=1024, name="ffn_prompt").reshape(B, S, D_MODEL)

    Bs, Ss, _ = x_sample.shape
    hist = (cache_k[l].reshape(Bs, WINDOW, KV_WIDTH), cache_v[l].reshape(Bs, WINDOW, KV_WIDTH), state_conv[l])
    xs1, ks, vs, cs = _mixer(x_sample, hist, rel_bias_table, sink, mixer_params, nb=8, tb=Ss, ch=Ss)
    ys = _ffn(xs1.reshape(Bs * Ss, D_MODEL), ffn_params, tm=1024, name="ffn_sample").reshape(Bs, Ss, D_MODEL)

    kv5 = lambda a: a.reshape(1, a.shape[0], WINDOW, N_KV_HEADS, HEAD_DIM)
    return (yp, ys, kv5(kp), kv5(vp), cp[None], kv5(ks), kv5(vs), cs[None])
```

```python
import functools
import math

import jax
import jax.numpy as jnp
from jax import lax
from jax.experimental import pallas as pl
from jax.experimental.pallas import tpu as pltpu

D_MODEL = 1024
HEAD_DIM = 64
N_HEADS = 16
N_KV_HEADS = 2
GROUP = N_HEADS // N_KV_HEADS
PAIRS = GROUP // 2
KV_WIDTH = N_KV_HEADS * HEAD_DIM
WINDOW = 128
CHUNK = 64
KEY_PAD = 256
CONV_WIDTH = 31
CONV_STATE = CONV_WIDTH - 1
CONV_HIST = 32
CONV_ROWS = 32
CONV_GROUP = 4
N_COL_BLOCKS = D_MODEL // 128
ROW_BLOCK = 16
FFN_HALVES = 2
D_FF = 2816
N_BUCKETS = 32
MAX_DISTANCE = 128
EPS = 1e-6
NEG_INF = -1e30
LOG2E = math.log2(math.e)
IN_WIDTH = D_MODEL + 2 * KV_WIDTH + 2 * D_MODEL + 2 * D_MODEL

_Q0 = 0
_K0 = D_MODEL
_V0 = _K0 + KV_WIDTH
_UA0 = _V0 + KV_WIDTH
_UB0 = _UA0 + D_MODEL
_GA0 = _UB0 + D_MODEL

VMEM_LIMIT_BYTES = 56 * 1024 * 1024


def _t5_bucket(rel):
    half = N_BUCKETS // 2
    max_exact = half // 2
    ret = jnp.where(rel > 0, half, 0)
    n = jnp.abs(rel)
    nf = jnp.maximum(n, 1).astype(jnp.float32)
    large = max_exact + (jnp.log(nf / max_exact) / math.log(MAX_DISTANCE / max_exact)
                         * (half - max_exact)).astype(jnp.int32)
    large = jnp.minimum(large, half - 1)
    return ret + jnp.where(n < max_exact, n, large)


def _rms(x, g):
    return x * lax.rsqrt(jnp.mean(x * x, axis=-1, keepdims=True) + EPS) * g


def _bf16_dot(a, w):
    return jnp.dot(a.astype(jnp.bfloat16), w, preferred_element_type=jnp.float32)


def _mixer_kernel(*refs, nb, tb, ch, carry_from_inputs):
    it = iter(refs)
    table_ref = next(it)
    sink_ref = next(it)
    bucket_ref = next(it)
    x_ref = next(it)
    if carry_from_inputs:
        ck_ref = next(it)
        cv_ref = next(it)
        cs_ref = next(it)
    g_pre_ref = next(it)
    w_in_ref = next(it)
    w_dw_ref = next(it)
    b_dw_ref = next(it)
    ln_g_ref = next(it)
    ln_b_ref = next(it)
    w_co_ref = next(it)
    w_out_ref = next(it)
    g_post_ref = next(it)
    x1_ref = next(it)
    kwin_ref = next(it)
    vwin_ref = next(it)
    cst_ref = next(it)
    bias_scr = next(it)
    kx_scr = next(it)
    vx_scr = next(it)
    u_scr = next(it)
    ue_scr = next(it)
    uo_scr = next(it)
    attn_scr = next(it)
    conv_scr = next(it)
    convo_scr = next(it)
    gate_scr = next(it)
    h_scr = next(it)
    q_scr = next(it)
    ya_scr = next(it)
    mix_scr = next(it)
    proj_scr = next(it)

    f32, bf16 = jnp.float32, jnp.bfloat16
    n_chunks = tb // ch
    key_rows = kx_scr.shape[3]
    first_step = (pl.program_id(0) == 0) & (pl.program_id(1) == 0)
    seq_tile = pl.program_id(1)
    low_lanes = lax.broadcasted_iota(jnp.int32, (1, KV_WIDTH), 1) < HEAD_DIM

    @pl.when(first_step)
    def _first_step():
        bucket = bucket_ref[...]
        pad_bias = jnp.where(bucket < 0, NEG_INF, 0.0)
        for h in range(N_HEADS):
            bias_scr[h] = pad_bias

        def body(b, carry):
            hit = bucket == b
            for h in range(N_HEADS):
                bias_scr[h] = jnp.where(hit, table_ref[b, h] * LOG2E, bias_scr[h])
            return carry

        lax.fori_loop(0, N_BUCKETS, body, 0)
        kx_scr[...] = jnp.zeros_like(kx_scr)
        vx_scr[:, :, :, :, 0:KV_WIDTH] = jnp.zeros(vx_scr.shape[:4] + (KV_WIDTH,), bf16)
        vx_scr[:, :, :, :, KV_WIDTH:2 * KV_WIDTH] = jnp.ones(vx_scr.shape[:4] + (KV_WIDTH,), bf16)
        u_scr[:, :, CONV_HIST + tb:CONV_HIST + tb + 8, :] = jnp.zeros((nb, N_COL_BLOCKS, 8, 128), f32)

    def store_kv(b, row0, k, v):
        n = k.shape[0]
        k_sw = pltpu.roll(k, HEAD_DIM, axis=1)
        v_sw = pltpu.roll(v, HEAD_DIM, axis=1)
        zero = jnp.zeros((), f32)
        for kh, (same, swapped) in enumerate(((k, k_sw), (k_sw, k))):
            kx_scr[b, kh, 0, row0:row0 + n, :] = jnp.where(low_lanes, same, zero).astype(bf16)
            kx_scr[b, kh, 1, row0:row0 + n, :] = jnp.where(low_lanes, zero, swapped).astype(bf16)
        for kh, (same, swapped) in enumerate(((v, v_sw), (v_sw, v))):
            vx_scr[b, kh, 0, row0:row0 + n, 0:KV_WIDTH] = jnp.where(low_lanes, same, zero).astype(bf16)
            vx_scr[b, kh, 1, row0:row0 + n, 0:KV_WIDTH] = jnp.where(low_lanes, zero, swapped).astype(bf16)

    if carry_from_inputs:
        for b in range(nb):
            store_kv(b, 0, ck_ref[b], cv_ref[b])
            for j in range(N_COL_BLOCKS):
                u_scr[b, j, 0:8, :] = jnp.zeros((8, 128), f32)
                u_scr[b, j, CONV_HIST - CONV_STATE:CONV_HIST, :] = cs_ref[b, :, j * 128:(j + 1) * 128]
    else:
        @pl.when(seq_tile == 0)
        def _reset_history():
            for b in range(nb):
                for kh in range(N_KV_HEADS):
                    for par in range(2):
                        kx_scr[b, kh, par, 0:WINDOW, :] = jnp.zeros((WINDOW, KV_WIDTH), bf16)
                        vx_scr[b, kh, par, 0:WINDOW, 0:KV_WIDTH] = jnp.zeros((WINDOW, KV_WIDTH), bf16)
                u_scr[b, :, 0:CONV_HIST, :] = jnp.zeros((N_COL_BLOCKS, CONV_HIST, 128), f32)

        @pl.when(seq_tile != 0)
        def _roll_history():
            for b in range(nb):
                for kh in range(N_KV_HEADS):
                    for par in range(2):
                        kx_scr[b, kh, par, 0:WINDOW, :] = kx_scr[b, kh, par, tb:tb + WINDOW, :]
                        vx_scr[b, kh, par, 0:WINDOW, 0:KV_WIDTH] = vx_scr[b, kh, par, tb:tb + WINDOW, 0:KV_WIDTH]
                u_scr[b, :, 0:CONV_HIST, :] = u_scr[b, :, tb:tb + CONV_HIST, :]

    m_rows = nb * tb

    def row_blocks():
        for r0 in range(0, m_rows, ROW_BLOCK):
            b, t0 = divmod(r0, tb)
            yield r0, b, t0

    for r0, b, t0 in row_blocks():
        xb = x_ref[b, t0:t0 + ROW_BLOCK, :]
        h_scr[r0:r0 + ROW_BLOCK, :] = _rms(xb, g_pre_ref[...]).astype(bf16)

    def proj(c0, n):
        return jnp.dot(h_scr[...], w_in_ref[:, c0:c0 + n], preferred_element_type=f32)

    def do_q(i):
        c0 = i * 512
        q_scr[:, c0:c0 + 512] = (proj(_Q0 + c0, 512) * (HEAD_DIM ** -0.5 * LOG2E)).astype(bf16)

    def do_kv():
        kv = proj(_K0, 2 * KV_WIDTH)
        k = kv[:, 0:KV_WIDTH]
        v = kv[:, KV_WIDTH:2 * KV_WIDTH]
        for b in range(nb):
            rows = slice(b * tb, (b + 1) * tb)
            store_kv(b, WINDOW, k[rows], v[rows])
            if tb >= WINDOW:
                kwin_ref[b] = k[b * tb + tb - WINDOW:(b + 1) * tb]
                vwin_ref[b] = v[b * tb + tb - WINDOW:(b + 1) * tb]
            else:
                kwin_ref[b, 0:WINDOW - tb, :] = ck_ref[b, tb:WINDOW, :]
                vwin_ref[b, 0:WINDOW - tb, :] = cv_ref[b, tb:WINDOW, :]
                kwin_ref[b, WINDOW - tb:WINDOW, :] = k[rows]
                vwin_ref[b, WINDOW - tb:WINDOW, :] = v[rows]

    def do_gate(i):
        c0 = i * 256
        gate_scr[:, c0:c0 + 256] = jax.nn.sigmoid(proj(_GA0 + c0, 256))

    side_tasks = [functools.partial(do_q, 0), functools.partial(do_q, 1), do_kv]
    side_tasks += [functools.partial(do_gate, i) for i in range(2 * D_MODEL // 256)]

    def glu_block(j):
        z = proj(_UA0 + j * 256, 256)
        return z[:, 0:128] * jax.nn.sigmoid(z[:, 128:256])

    units_per_block = nb * (tb // CONV_ROWS)
    total_units = N_COL_BLOCKS * units_per_block
    first_side = 2 * units_per_block
    side_after = [first_side + (k * (total_units - first_side)) // len(side_tasks) for k in range(len(side_tasks))]
    next_side = 0
    unit = 0
    u_next = glu_block(0)
    for j in range(N_COL_BLOCKS):
        u_cur = u_next
        cols = slice(j * 128, (j + 1) * 128)
        for b in range(nb):
            u_scr[b, j, CONV_HIST:CONV_HIST + tb, :] = u_cur[b * tb:(b + 1) * tb, :]
            cst_ref[b, :, cols] = u_scr[b, j, CONV_HIST + tb - CONV_STATE:CONV_HIST + tb, :]
            n_rows = CONV_HIST + tb
            ue_scr[b, j] = pltpu.bitcast(u_scr[b, j, 0:n_rows, :].astype(bf16), jnp.uint32)
            uo_scr[b, j] = pltpu.bitcast(u_scr[b, j, 1:n_rows + 1, :].astype(bf16), jnp.uint32)
        if j + 1 < N_COL_BLOCKS:
            u_next = glu_block(j + 1)
        for b in range(nb):
            for r in range(tb // CONV_ROWS):
                base = r * CONV_ROWS
                acc = jnp.broadcast_to(b_dw_ref[:, cols], (CONV_ROWS, 128))
                for t0 in range(0, CONV_WIDTH, CONV_GROUP):
                    terms = []
                    for t in range(t0, min(t0 + CONV_GROUP, CONV_WIDTH)):
                        off = base + CONV_HIST - CONV_STATE + t
                        src = ue_scr if off % 2 == 0 else uo_scr
                        win = pltpu.bitcast(src[b, j, off // 2:off // 2 + CONV_ROWS // 2, :], bf16)
                        w_t = pltpu.bitcast(jnp.broadcast_to(w_dw_ref[t:t + 1, cols], (CONV_ROWS // 2, 128)), bf16)
                        terms.append(w_t * win)
                    while len(terms) > 1:
                        terms = [terms[i] + terms[i + 1] if i + 1 < len(terms) else terms[i]
                                 for i in range(0, len(terms), 2)]
                    acc = acc + terms[0].astype(f32)
                conv_scr[b * tb + base:b * tb + base + CONV_ROWS, cols] = acc
                unit += 1
                while next_side < len(side_tasks) and side_after[next_side] <= unit:
                    side_tasks[next_side]()
                    next_side += 1
    while next_side < len(side_tasks):
        side_tasks[next_side]()
        next_side += 1

    key_iota = lax.broadcasted_iota(jnp.int32, (ch, KEY_PAD), 1)
    groups = [(b, c, kh) for b in range(nb) for c in range(n_chunks) for kh in range(N_KV_HEADS)]

    def scores(b, c, kh):
        r0 = b * tb + c * ch
        qp = jnp.concatenate(
            [q_scr[r0:r0 + ch, (kh * PAIRS + i) * 128:(kh * PAIRS + i + 1) * 128] for i in range(PAIRS)], axis=0)
        out = []
        for par in range(2):
            kx = kx_scr[b, kh, par, c * ch:c * ch + KEY_PAD, :]
            out.append(lax.dot_general(qp, kx, (((1,), (1,)), ((), ())), preferred_element_type=f32))
        return out

    def softmax_pv(b, c, kh, s_pair):
        r0 = b * tb + c * ch
        need_mask = (not carry_from_inputs) and (c * ch < WINDOW)
        if need_mask:
            first_pos = seq_tile * tb + (c * ch - WINDOW)
            valid = (key_iota + first_pos) >= 0
        o_pair = []
        sink_terms = []
        for par in range(2):
            ps = []
            for i in range(PAIRS):
                hh = kh * GROUP + 2 * i + par
                sg = s_pair[par][i * ch:(i + 1) * ch] + bias_scr[hh]
                if need_mask:
                    sg = jnp.where(valid, sg, NEG_INF)
                sk = sink_ref[hh] * LOG2E
                m = jnp.maximum(jnp.max(sg, axis=-1, keepdims=True), sk)
                ps.append(jnp.exp2(sg - m).astype(bf16))
                sink_terms.append(jnp.exp2(sk - m))
            pcat = jnp.concatenate(ps, axis=0)
            vx = vx_scr[b, kh, par, c * ch:c * ch + KEY_PAD, :]
            o_pair.append(jnp.dot(pcat, vx, preferred_element_type=f32))
        for i in range(PAIRS):
            rows = slice(i * ch, (i + 1) * ch)
            blk = None
            for par in range(2):
                o = o_pair[par]
                den = o[rows, KV_WIDTH:2 * KV_WIDTH] + sink_terms[par * PAIRS + i]
                term = o[rows, 0:KV_WIDTH] * (1.0 / den)
                blk = term if blk is None else blk + term
            c0 = (kh * PAIRS + i) * 128
            attn_scr[r0:r0 + ch, c0:c0 + 128] = blk

    s_next = scores(*groups[0])
    for r0, b, t0 in row_blocks():
        y = conv_scr[r0:r0 + ROW_BLOCK, :]
        mu = jnp.mean(y, axis=-1, keepdims=True)
        yc = y - mu
        yn = yc * lax.rsqrt(jnp.mean(yc * yc, axis=-1, keepdims=True) + EPS) * ln_g_ref[...] + ln_b_ref[...]
        ya_scr[r0:r0 + ROW_BLOCK, :] = (yn * jax.nn.sigmoid(yn)).astype(bf16)

    n_co_chunks = D_MODEL // 256
    co_after = [((k + 1) * len(groups)) // (n_co_chunks + 1) for k in range(n_co_chunks)]
    next_co = 0
    for gi, grp in enumerate(groups):
        s_cur = s_next
        if gi + 1 < len(groups):
            s_next = scores(*groups[gi + 1])
        softmax_pv(*grp, s_cur)
        while next_co < n_co_chunks and co_after[next_co] <= gi + 1:
            c0 = next_co * 256
            convo_scr[:, c0:c0 + 256] = jnp.dot(ya_scr[...], w_co_ref[:, c0:c0 + 256], preferred_element_type=f32)
            next_co += 1

    for r0, b, t0 in row_blocks():
        rows = slice(r0, r0 + ROW_BLOCK)
        mix = gate_scr[rows, 0:D_MODEL] * attn_scr[rows, :] + gate_scr[rows, D_MODEL:2 * D_MODEL] * convo_scr[rows, :]
        mix_scr[rows, :] = mix.astype(bf16)
    for c0 in range(0, D_MODEL, 256):
        proj_scr[:, c0:c0 + 256] = jnp.dot(mix_scr[...], w_out_ref[:, c0:c0 + 256], preferred_element_type=f32)
    for r0, b, t0 in row_blocks():
        xb = x_ref[b, t0:t0 + ROW_BLOCK, :]
        x1_ref[b, t0:t0 + ROW_BLOCK, :] = xb + _rms(proj_scr[r0:r0 + ROW_BLOCK, :], g_post_ref[...])


def _const_spec(shape):
    zeros = (0,) * len(shape)
    return pl.BlockSpec(shape, lambda i, j: zeros, pipeline_mode=pl.Buffered(1))


def _mixer(x, hist, table, sink, params, *, nb, tb, ch):
    B, S, _ = x.shape
    carry_from_inputs = hist is not None
    nkeys = WINDOW + ch
    assert nkeys <= KEY_PAD
    rel = jnp.arange(nkeys)[None, :] - WINDOW - jnp.arange(ch)[:, None]
    bucket = jnp.pad(_t5_bucket(rel).astype(jnp.int32), ((0, 0), (0, KEY_PAD - nkeys)), constant_values=-1)
    g_pre, w_in, w_dw, b_dw, ln_g, ln_b, w_co, w_out, g_post = params
    key_rows = (tb // ch - 1) * ch + KEY_PAD

    grid = (B // nb, S // tb)
    smem = pl.BlockSpec(memory_space=pltpu.SMEM)
    in_specs = [smem, smem, _const_spec((ch, KEY_PAD)),
                pl.BlockSpec((nb, tb, D_MODEL), lambda i, j: (i, j, 0))]
    args = [table, sink, bucket, x]
    if carry_from_inputs:
        in_specs += [pl.BlockSpec((nb, WINDOW, KV_WIDTH), lambda i, j: (i, 0, 0)),
                     pl.BlockSpec((nb, WINDOW, KV_WIDTH), lambda i, j: (i, 0, 0)),
                     pl.BlockSpec((nb, CONV_STATE, D_MODEL), lambda i, j: (i, 0, 0))]
        args += list(hist)
    weights = [g_pre, w_in, w_dw, b_dw, ln_g, ln_b, w_co, w_out, g_post]
    in_specs += [_const_spec(w.shape) for w in weights]
    args += weights

    out_shape = (jax.ShapeDtypeStruct((B, S, D_MODEL), jnp.float32),
                 jax.ShapeDtypeStruct((B, WINDOW, KV_WIDTH), jnp.float32),
                 jax.ShapeDtypeStruct((B, WINDOW, KV_WIDTH), jnp.float32),
                 jax.ShapeDtypeStruct((B, CONV_STATE, D_MODEL), jnp.float32))
    out_specs = (pl.BlockSpec((nb, tb, D_MODEL), lambda i, j: (i, j, 0)),
                 pl.BlockSpec((nb, WINDOW, KV_WIDTH), lambda i, j: (i, 0, 0)),
                 pl.BlockSpec((nb, WINDOW, KV_WIDTH), lambda i, j: (i, 0, 0)),
                 pl.BlockSpec((nb, CONV_STATE, D_MODEL), lambda i, j: (i, 0, 0)))
    f32, bf16 = jnp.float32, jnp.bfloat16
    scratch = [pltpu.VMEM((N_HEADS, ch, KEY_PAD), f32),
               pltpu.VMEM((nb, N_KV_HEADS, 2, key_rows, KV_WIDTH), bf16),
               pltpu.VMEM((nb, N_KV_HEADS, 2, key_rows, 2 * KV_WIDTH), bf16),
               pltpu.VMEM((nb, N_COL_BLOCKS, CONV_HIST + tb + 8, 128), f32),
               pltpu.VMEM((nb, N_COL_BLOCKS, (CONV_HIST + tb) // 2, 128), jnp.uint32),
               pltpu.VMEM((nb, N_COL_BLOCKS, (CONV_HIST + tb) // 2, 128), jnp.uint32),
               pltpu.VMEM((nb * tb, D_MODEL), f32),
               pltpu.VMEM((nb * tb, D_MODEL), f32),
               pltpu.VMEM((nb * tb, D_MODEL), f32),
               pltpu.VMEM((nb * tb, 2 * D_MODEL), f32),
               pltpu.VMEM((nb * tb, D_MODEL), bf16),
               pltpu.VMEM((nb * tb, D_MODEL), bf16),
               pltpu.VMEM((nb * tb, D_MODEL), bf16),
               pltpu.VMEM((nb * tb, D_MODEL), bf16),
               pltpu.VMEM((nb * tb, D_MODEL), f32)]
    kern = functools.partial(_mixer_kernel, nb=nb, tb=tb, ch=ch, carry_from_inputs=carry_from_inputs)
    return pl.pallas_call(
        kern, grid=grid, in_specs=in_specs, out_specs=out_specs, out_shape=out_shape,
        scratch_shapes=scratch,
        compiler_params=pltpu.CompilerParams(dimension_semantics=("arbitrary", "arbitrary"),
                                             vmem_limit_bytes=VMEM_LIMIT_BYTES),
        name="mixer_sample" if carry_from_inputs else "mixer_prompt",
    )(*args)


def _ffn_kernel(x_ref, g_pre_ref, wg_ref, wu_ref, wd_ref, g_post_ref, y_ref):
    tm = x_ref.shape[0]
    hm = tm // FFN_HALVES
    hs = [_rms(x_ref[i * hm:(i + 1) * hm, :], g_pre_ref[...]).astype(jnp.bfloat16) for i in range(FFN_HALVES)]
    for i in range(FFN_HALVES):
        rows = slice(i * hm, (i + 1) * hm)
        a = jnp.dot(hs[i], wg_ref[...], preferred_element_type=jnp.float32)
        b = jnp.dot(hs[i], wu_ref[...], preferred_element_type=jnp.float32)
        act = a * jax.nn.sigmoid(a) * b
        f = _bf16_dot(act, wd_ref[...])
        y_ref[rows, :] = x_ref[rows, :] + _rms(f, g_post_ref[...])


def _ffn(x, params, *, tm, name):
    n = x.shape[0]
    const = lambda shape: pl.BlockSpec(shape, lambda i: (0,) * len(shape), pipeline_mode=pl.Buffered(1))
    return pl.pallas_call(
        _ffn_kernel, grid=(n // tm,),
        in_specs=[pl.BlockSpec((tm, D_MODEL), lambda i: (i, 0))] + [const(w.shape) for w in params],
        out_specs=pl.BlockSpec((tm, D_MODEL), lambda i: (i, 0)),
        out_shape=jax.ShapeDtypeStruct((n, D_MODEL), jnp.float32),
        compiler_params=pltpu.CompilerParams(dimension_semantics=("arbitrary",),
                                             vmem_limit_bytes=VMEM_LIMIT_BYTES),
        name=name,
    )(x, *params)


def kernel(x_prompt, x_sample, cache_k, cache_v, state_conv, rel_bias_table, attn_sink, g_pre_mix, w_in, w_dw, b_dw, ln_g, ln_b, w_conv_out, w_out, g_post_mix, g_pre_ffn, w_ffn_gate, w_ffn_up, w_ffn_down, g_post_ffn):
    depth = w_in.shape[0]
    assert depth == 1, "single-layer step"
    bf = jnp.bfloat16
    row = lambda a: a.reshape(1, -1)
    l = 0
    w_ua = w_in[l][:, _UA0:_UB0].reshape(D_MODEL, N_COL_BLOCKS, 1, 128)
    w_ub = w_in[l][:, _UB0:_GA0].reshape(D_MODEL, N_COL_BLOCKS, 1, 128)
    w_glu = jnp.concatenate([w_ua, w_ub], axis=2).reshape(D_MODEL, 2 * D_MODEL)
    w_in_k = jnp.concatenate([w_in[l][:, :_UA0], w_glu, w_in[l][:, _GA0:]], axis=1).astype(bf)
    w_dw_bf = w_dw[l].astype(bf)
    w_dw_k = lax.bitcast_convert_type(jnp.stack([w_dw_bf, w_dw_bf], axis=-1), jnp.uint32)
    mixer_params = (row(g_pre_mix[l]), w_in_k, w_dw_k, row(b_dw[l]), row(ln_g[l]), row(ln_b[l]),
                    w_conv_out[l].astype(bf), w_out[l].astype(bf), row(g_post_mix[l]))
    ffn_params = (row(g_pre_ffn[l]), w_ffn_gate[l].astype(bf), w_ffn_up[l].astype(bf),
                  w_ffn_down[l].astype(bf), row(g_post_ffn[l]))
    sink = attn_sink[l]

    B, S, _ = x_prompt.shape
    xp1, kp, vp, cp = _mixer(x_prompt, None, rel_bias_table, sink, mixer_params, nb=1, tb=512, ch=CHUNK)
    yp = _ffn(xp1.reshape(B * S, D_MODEL), ffn_params, tm=1024, name="ffn_prompt").reshape(B, S, D_MODEL)

    Bs, Ss, _ = x_sample.shape
    hist = (cache_k[l].reshape(Bs, WINDOW, KV_WIDTH), cache_v[l].reshape(Bs, WINDOW, KV_WIDTH), state_conv[l])
    xs1, ks, vs, cs = _mixer(x_sample, hist, rel_bias_table, sink, mixer_params, nb=8, tb=Ss, ch=Ss)
    ys = _ffn(xs1.reshape(Bs * Ss, D_MODEL), ffn_params, tm=1024, name="ffn_sample").reshape(Bs, Ss, D_MODEL)

    kv5 = lambda a: a.reshape(1, a.shape[0], WINDOW, N_KV_HEADS, HEAD_DIM)
    return (yp, ys, kv5(kp), kv5(vp), cp[None], kv5(ks), kv5(vs), cs[None])
```

```python
import functools
import math

import jax
import jax.numpy as jnp
from jax import lax
from jax.experimental import pallas as pl
from jax.experimental.pallas import tpu as pltpu

D_MODEL = 1024
HEAD_DIM = 64
N_HEADS = 16
N_KV_HEADS = 2
GROUP = N_HEADS // N_KV_HEADS
PAIRS = GROUP // 2
KV_WIDTH = N_KV_HEADS * HEAD_DIM
WINDOW = 128
CHUNK = 64
KEY_PAD = 256
CONV_WIDTH = 31
CONV_STATE = CONV_WIDTH - 1
CONV_HIST = 32
CONV_ROWS = 32
N_COL_BLOCKS = D_MODEL // 128
ROW_BLOCK = 16
FFN_HALVES = 2
D_FF = 2816
N_BUCKETS = 32
MAX_DISTANCE = 128
EPS = 1e-6
NEG_INF = -1e30
IN_WIDTH = D_MODEL + 2 * KV_WIDTH + 2 * D_MODEL + 2 * D_MODEL

_Q0 = 0
_K0 = D_MODEL
_V0 = _K0 + KV_WIDTH
_UA0 = _V0 + KV_WIDTH
_UB0 = _UA0 + D_MODEL
_GA0 = _UB0 + D_MODEL

VMEM_LIMIT_BYTES = 56 * 1024 * 1024


def _t5_bucket(rel):
    half = N_BUCKETS // 2
    max_exact = half // 2
    ret = jnp.where(rel > 0, half, 0)
    n = jnp.abs(rel)
    nf = jnp.maximum(n, 1).astype(jnp.float32)
    large = max_exact + (jnp.log(nf / max_exact) / math.log(MAX_DISTANCE / max_exact)
                         * (half - max_exact)).astype(jnp.int32)
    large = jnp.minimum(large, half - 1)
    return ret + jnp.where(n < max_exact, n, large)


def _rms(x, g):
    return x * lax.rsqrt(jnp.mean(x * x, axis=-1, keepdims=True) + EPS) * g


def _bf16_dot(a, w):
    return jnp.dot(a.astype(jnp.bfloat16), w, preferred_element_type=jnp.float32)


def _mixer_kernel(*refs, nb, tb, ch, carry_from_inputs):
    it = iter(refs)
    table_ref = next(it)
    sink_ref = next(it)
    bucket_ref = next(it)
    x_ref = next(it)
    if carry_from_inputs:
        ck_ref = next(it)
        cv_ref = next(it)
        cs_ref = next(it)
    g_pre_ref = next(it)
    w_in_ref = next(it)
    w_dw_ref = next(it)
    b_dw_ref = next(it)
    ln_g_ref = next(it)
    ln_b_ref = next(it)
    w_co_ref = next(it)
    w_out_ref = next(it)
    g_post_ref = next(it)
    x1_ref = next(it)
    kwin_ref = next(it)
    vwin_ref = next(it)
    cst_ref = next(it)
    bias_scr = next(it)
    kx_scr = next(it)
    vx_scr = next(it)
    u_scr = next(it)
    attn_scr = next(it)
    conv_scr = next(it)
    convo_scr = next(it)
    gate_scr = next(it)
    h_scr = next(it)
    q_scr = next(it)
    ya_scr = next(it)
    mix_scr = next(it)
    proj_scr = next(it)

    f32, bf16 = jnp.float32, jnp.bfloat16
    n_chunks = tb // ch
    key_rows = kx_scr.shape[3]
    first_step = (pl.program_id(0) == 0) & (pl.program_id(1) == 0)
    seq_tile = pl.program_id(1)
    low_lanes = lax.broadcasted_iota(jnp.int32, (1, KV_WIDTH), 1) < HEAD_DIM

    @pl.when(first_step)
    def _first_step():
        bucket = bucket_ref[...]
        pad_bias = jnp.where(bucket < 0, NEG_INF, 0.0)
        for h in range(N_HEADS):
            bias_scr[h] = pad_bias

        def body(b, carry):
            hit = bucket == b
            for h in range(N_HEADS):
                bias_scr[h] = jnp.where(hit, table_ref[b, h], bias_scr[h])
            return carry

        lax.fori_loop(0, N_BUCKETS, body, 0)
        kx_scr[...] = jnp.zeros_like(kx_scr)
        vx_scr[:, :, :, :, 0:KV_WIDTH] = jnp.zeros(vx_scr.shape[:4] + (KV_WIDTH,), bf16)
        vx_scr[:, :, :, :, KV_WIDTH:2 * KV_WIDTH] = jnp.ones(vx_scr.shape[:4] + (KV_WIDTH,), bf16)

    def store_kv(b, row0, k, v):
        n = k.shape[0]
        k_sw = pltpu.roll(k, HEAD_DIM, axis=1)
        v_sw = pltpu.roll(v, HEAD_DIM, axis=1)
        zero = jnp.zeros((), f32)
        for kh, (same, swapped) in enumerate(((k, k_sw), (k_sw, k))):
            kx_scr[b, kh, 0, row0:row0 + n, :] = jnp.where(low_lanes, same, zero).astype(bf16)
            kx_scr[b, kh, 1, row0:row0 + n, :] = jnp.where(low_lanes, zero, swapped).astype(bf16)
        for kh, (same, swapped) in enumerate(((v, v_sw), (v_sw, v))):
            vx_scr[b, kh, 0, row0:row0 + n, 0:KV_WIDTH] = jnp.where(low_lanes, same, zero).astype(bf16)
            vx_scr[b, kh, 1, row0:row0 + n, 0:KV_WIDTH] = jnp.where(low_lanes, zero, swapped).astype(bf16)

    if carry_from_inputs:
        for b in range(nb):
            store_kv(b, 0, ck_ref[b], cv_ref[b])
            for j in range(N_COL_BLOCKS):
                u_scr[b, j, 0:8, :] = jnp.zeros((8, 128), f32)
                u_scr[b, j, CONV_HIST - CONV_STATE:CONV_HIST, :] = cs_ref[b, :, j * 128:(j + 1) * 128]
    else:
        @pl.when(seq_tile == 0)
        def _reset_history():
            for b in range(nb):
                for kh in range(N_KV_HEADS):
                    for par in range(2):
                        kx_scr[b, kh, par, 0:WINDOW, :] = jnp.zeros((WINDOW, KV_WIDTH), bf16)
                        vx_scr[b, kh, par, 0:WINDOW, 0:KV_WIDTH] = jnp.zeros((WINDOW, KV_WIDTH), bf16)
                u_scr[b, :, 0:CONV_HIST, :] = jnp.zeros((N_COL_BLOCKS, CONV_HIST, 128), f32)

        @pl.when(seq_tile != 0)
        def _roll_history():
            for b in range(nb):
                for kh in range(N_KV_HEADS):
                    for par in range(2):
                        kx_scr[b, kh, par, 0:WINDOW, :] = kx_scr[b, kh, par, tb:tb + WINDOW, :]
                        vx_scr[b, kh, par, 0:WINDOW, 0:KV_WIDTH] = vx_scr[b, kh, par, tb:tb + WINDOW, 0:KV_WIDTH]
                u_scr[b, :, 0:CONV_HIST, :] = u_scr[b, :, tb:tb + CONV_HIST, :]

    m_rows = nb * tb

    def row_blocks():
        for r0 in range(0, m_rows, ROW_BLOCK):
            b, t0 = divmod(r0, tb)
            yield r0, b, t0

    for r0, b, t0 in row_blocks():
        xb = x_ref[b, t0:t0 + ROW_BLOCK, :]
        h_scr[r0:r0 + ROW_BLOCK, :] = _rms(xb, g_pre_ref[...]).astype(bf16)

    def proj(c0, n):
        return jnp.dot(h_scr[...], w_in_ref[:, c0:c0 + n], preferred_element_type=f32)

    def do_q(i):
        c0 = i * 512
        q_scr[:, c0:c0 + 512] = (proj(_Q0 + c0, 512) * (HEAD_DIM ** -0.5)).astype(bf16)

    def do_kv():
        kv = proj(_K0, 2 * KV_WIDTH)
        k = kv[:, 0:KV_WIDTH]
        v = kv[:, KV_WIDTH:2 * KV_WIDTH]
        for b in range(nb):
            rows = slice(b * tb, (b + 1) * tb)
            store_kv(b, WINDOW, k[rows], v[rows])
            if tb >= WINDOW:
                kwin_ref[b] = k[b * tb + tb - WINDOW:(b + 1) * tb]
                vwin_ref[b] = v[b * tb + tb - WINDOW:(b + 1) * tb]
            else:
                kwin_ref[b, 0:WINDOW - tb, :] = ck_ref[b, tb:WINDOW, :]
                vwin_ref[b, 0:WINDOW - tb, :] = cv_ref[b, tb:WINDOW, :]
                kwin_ref[b, WINDOW - tb:WINDOW, :] = k[rows]
                vwin_ref[b, WINDOW - tb:WINDOW, :] = v[rows]

    def do_gate(i):
        c0 = i * 256
        gate_scr[:, c0:c0 + 256] = jax.nn.sigmoid(proj(_GA0 + c0, 256))

    side_tasks = [functools.partial(do_q, 0), functools.partial(do_q, 1), do_kv]
    side_tasks += [functools.partial(do_gate, i) for i in range(2 * D_MODEL // 256)]

    def glu_block(j):
        z = proj(_UA0 + j * 256, 256)
        return z[:, 0:128] * jax.nn.sigmoid(z[:, 128:256])

    units_per_block = nb * (tb // CONV_ROWS)
    total_units = N_COL_BLOCKS * units_per_block
    first_side = 2 * units_per_block
    side_after = [first_side + (k * (total_units - first_side)) // len(side_tasks) for k in range(len(side_tasks))]
    next_side = 0
    unit = 0
    u_next = glu_block(0)
    for j in range(N_COL_BLOCKS):
        u_cur = u_next
        cols = slice(j * 128, (j + 1) * 128)
        for b in range(nb):
            u_scr[b, j, CONV_HIST:CONV_HIST + tb, :] = u_cur[b * tb:(b + 1) * tb, :]
            cst_ref[b, :, cols] = u_scr[b, j, CONV_HIST + tb - CONV_STATE:CONV_HIST + tb, :]
        if j + 1 < N_COL_BLOCKS:
            u_next = glu_block(j + 1)
        for b in range(nb):
            for r in range(tb // CONV_ROWS):
                base = r * CONV_ROWS
                acc = jnp.broadcast_to(b_dw_ref[:, cols], (CONV_ROWS, 128))
                for t in range(CONV_WIDTH):
                    off = base + CONV_HIST - CONV_STATE + t
                    acc = acc + w_dw_ref[t:t + 1, cols] * u_scr[b, j, off:off + CONV_ROWS, :]
                conv_scr[b * tb + base:b * tb + base + CONV_ROWS, cols] = acc
                unit += 1
                while next_side < len(side_tasks) and side_after[next_side] <= unit:
                    side_tasks[next_side]()
                    next_side += 1
    while next_side < len(side_tasks):
        side_tasks[next_side]()
        next_side += 1

    key_iota = lax.broadcasted_iota(jnp.int32, (ch, KEY_PAD), 1)
    groups = [(b, c, kh) for b in range(nb) for c in range(n_chunks) for kh in range(N_KV_HEADS)]

    def scores(b, c, kh):
        r0 = b * tb + c * ch
        qp = jnp.concatenate(
            [q_scr[r0:r0 + ch, (kh * PAIRS + i) * 128:(kh * PAIRS + i + 1) * 128] for i in range(PAIRS)], axis=0)
        out = []
        for par in range(2):
            kx = kx_scr[b, kh, par, c * ch:c * ch + KEY_PAD, :]
            out.append(lax.dot_general(qp, kx, (((1,), (1,)), ((), ())), preferred_element_type=f32))
        return out

    def softmax_pv(b, c, kh, s_pair):
        r0 = b * tb + c * ch
        need_mask = (not carry_from_inputs) and (c * ch < WINDOW)
        if need_mask:
            first_pos = seq_tile * tb + (c * ch - WINDOW)
            valid = (key_iota + first_pos) >= 0
        o_pair = []
        sink_terms = []
        for par in range(2):
            ps = []
            for i in range(PAIRS):
                hh = kh * GROUP + 2 * i + par
                sg = s_pair[par][i * ch:(i + 1) * ch] + bias_scr[hh]
                if need_mask:
                    sg = jnp.where(valid, sg, NEG_INF)
                sk = sink_ref[hh]
                m = jnp.maximum(jnp.max(sg, axis=-1, keepdims=True), sk)
                ps.append(jnp.exp(sg - m).astype(bf16))
                sink_terms.append(jnp.exp(sk - m))
            pcat = jnp.concatenate(ps, axis=0)
            vx = vx_scr[b, kh, par, c * ch:c * ch + KEY_PAD, :]
            o_pair.append(jnp.dot(pcat, vx, preferred_element_type=f32))
        for i in range(PAIRS):
            rows = slice(i * ch, (i + 1) * ch)
            blk = None
            for par in range(2):
                o = o_pair[par]
                den = o[rows, KV_WIDTH:2 * KV_WIDTH] + sink_terms[par * PAIRS + i]
                term = o[rows, 0:KV_WIDTH] * (1.0 / den)
                blk = term if blk is None else blk + term
            c0 = (kh * PAIRS + i) * 128
            attn_scr[r0:r0 + ch, c0:c0 + 128] = blk

    s_next = scores(*groups[0])
    for r0, b, t0 in row_blocks():
        y = conv_scr[r0:r0 + ROW_BLOCK, :]
        mu = jnp.mean(y, axis=-1, keepdims=True)
        yc = y - mu
        yn = yc * lax.rsqrt(jnp.mean(yc * yc, axis=-1, keepdims=True) + EPS) * ln_g_ref[...] + ln_b_ref[...]
        ya_scr[r0:r0 + ROW_BLOCK, :] = (yn * jax.nn.sigmoid(yn)).astype(bf16)

    n_co_chunks = D_MODEL // 256
    co_after = [((k + 1) * len(groups)) // (n_co_chunks + 1) for k in range(n_co_chunks)]
    next_co = 0
    for gi, grp in enumerate(groups):
        s_cur = s_next
        if gi + 1 < len(groups):
            s_next = scores(*groups[gi + 1])
        softmax_pv(*grp, s_cur)
        while next_co < n_co_chunks and co_after[next_co] <= gi + 1:
            c0 = next_co * 256
            convo_scr[:, c0:c0 + 256] = jnp.dot(ya_scr[...], w_co_ref[:, c0:c0 + 256], preferred_element_type=f32)
            next_co += 1

    for r0, b, t0 in row_blocks():
        rows = slice(r0, r0 + ROW_BLOCK)
        mix = gate_scr[rows, 0:D_MODEL] * attn_scr[rows, :] + gate_scr[rows, D_MODEL:2 * D_MODEL] * convo_scr[rows, :]
        mix_scr[rows, :] = mix.astype(bf16)
    for c0 in range(0, D_MODEL, 256):
        proj_scr[:, c0:c0 + 256] = jnp.dot(mix_scr[...], w_out_ref[:, c0:c0 + 256], preferred_element_type=f32)
    for r0, b, t0 in row_blocks():
        xb = x_ref[b, t0:t0 + ROW_BLOCK, :]
        x1_ref[b, t0:t0 + ROW_BLOCK, :] = xb + _rms(proj_scr[r0:r0 + ROW_BLOCK, :], g_post_ref[...])


def _const_spec(shape):
    zeros = (0,) * len(shape)
    return pl.BlockSpec(shape, lambda i, j: zeros, pipeline_mode=pl.Buffered(1))


def _mixer(x, hist, table, sink, params, *, nb, tb, ch):
    B, S, _ = x.shape
    carry_from_inputs = hist is not None
    nkeys = WINDOW + ch
    assert nkeys <= KEY_PAD
    rel = jnp.arange(nkeys)[None, :] - WINDOW - jnp.arange(ch)[:, None]
    bucket = jnp.pad(_t5_bucket(rel).astype(jnp.int32), ((0, 0), (0, KEY_PAD - nkeys)), constant_values=-1)
    g_pre, w_in, w_dw, b_dw, ln_g, ln_b, w_co, w_out, g_post = params
    key_rows = (tb // ch - 1) * ch + KEY_PAD

    grid = (B // nb, S // tb)
    smem = pl.BlockSpec(memory_space=pltpu.SMEM)
    in_specs = [smem, smem, _const_spec((ch, KEY_PAD)),
                pl.BlockSpec((nb, tb, D_MODEL), lambda i, j: (i, j, 0))]
    args = [table, sink, bucket, x]
    if carry_from_inputs:
        in_specs += [pl.BlockSpec((nb, WINDOW, KV_WIDTH), lambda i, j: (i, 0, 0)),
                     pl.BlockSpec((nb, WINDOW, KV_WIDTH), lambda i, j: (i, 0, 0)),
                     pl.BlockSpec((nb, CONV_STATE, D_MODEL), lambda i, j: (i, 0, 0))]
        args += list(hist)
    weights = [g_pre, w_in, w_dw, b_dw, ln_g, ln_b, w_co, w_out, g_post]
    in_specs += [_const_spec(w.shape) for w in weights]
    args += weights

    out_shape = (jax.ShapeDtypeStruct((B, S, D_MODEL), jnp.float32),
                 jax.ShapeDtypeStruct((B, WINDOW, KV_WIDTH), jnp.float32),
                 jax.ShapeDtypeStruct((B, WINDOW, KV_WIDTH), jnp.float32),
                 jax.ShapeDtypeStruct((B, CONV_STATE, D_MODEL), jnp.float32))
    out_specs = (pl.BlockSpec((nb, tb, D_MODEL), lambda i, j: (i, j, 0)),
                 pl.BlockSpec((nb, WINDOW, KV_WIDTH), lambda i, j: (i, 0, 0)),
                 pl.BlockSpec((nb, WINDOW, KV_WIDTH), lambda i, j: (i, 0, 0)),
                 pl.BlockSpec((nb, CONV_STATE, D_MODEL), lambda i, j: (i, 0, 0)))
    f32, bf16 = jnp.float32, jnp.bfloat16
    scratch = [pltpu.VMEM((N_HEADS, ch, KEY_PAD), f32),
               pltpu.VMEM((nb, N_KV_HEADS, 2, key_rows, KV_WIDTH), bf16),
               pltpu.VMEM((nb, N_KV_HEADS, 2, key_rows, 2 * KV_WIDTH), bf16),
               pltpu.VMEM((nb, N_COL_BLOCKS, CONV_HIST + tb, 128), f32),
               pltpu.VMEM((nb * tb, D_MODEL), f32),
               pltpu.VMEM((nb * tb, D_MODEL), f32),
               pltpu.VMEM((nb * tb, D_MODEL), f32),
               pltpu.VMEM((nb * tb, 2 * D_MODEL), f32),
               pltpu.VMEM((nb * tb, D_MODEL), bf16),
               pltpu.VMEM((nb * tb, D_MODEL), bf16),
               pltpu.VMEM((nb * tb, D_MODEL), bf16),
               pltpu.VMEM((nb * tb, D_MODEL), bf16),
               pltpu.VMEM((nb * tb, D_MODEL), f32)]
    kern = functools.partial(_mixer_kernel, nb=nb, tb=tb, ch=ch, carry_from_inputs=carry_from_inputs)
    return pl.pallas_call(
        kern, grid=grid, in_specs=in_specs, out_specs=out_specs, out_shape=out_shape,
        scratch_shapes=scratch,
        compiler_params=pltpu.CompilerParams(dimension_semantics=("arbitrary", "arbitrary"),
                                             vmem_limit_bytes=VMEM_LIMIT_BYTES),
        name="mixer_sample" if carry_from_inputs else "mixer_prompt",
    )(*args)


def _ffn_kernel(x_ref, g_pre_ref, wg_ref, wu_ref, wd_ref, g_post_ref, y_ref):
    tm = x_ref.shape[0]
    hm = tm // FFN_HALVES
    hs = [_rms(x_ref[i * hm:(i + 1) * hm, :], g_pre_ref[...]).astype(jnp.bfloat16) for i in range(FFN_HALVES)]
    for i in range(FFN_HALVES):
        rows = slice(i * hm, (i + 1) * hm)
        a = jnp.dot(hs[i], wg_ref[...], preferred_element_type=jnp.float32)
        b = jnp.dot(hs[i], wu_ref[...], preferred_element_type=jnp.float32)
        act = a * jax.nn.sigmoid(a) * b
        f = _bf16_dot(act, wd_ref[...])
        y_ref[rows, :] = x_ref[rows, :] + _rms(f, g_post_ref[...])


def _ffn(x, params, *, tm, name):
    n = x.shape[0]
    const = lambda shape: pl.BlockSpec(shape, lambda i: (0,) * len(shape), pipeline_mode=pl.Buffered(1))
    return pl.pallas_call(
        _ffn_kernel, grid=(n // tm,),
        in_specs=[pl.BlockSpec((tm, D_MODEL), lambda i: (i, 0))] + [const(w.shape) for w in params],
        out_specs=pl.BlockSpec((tm, D_MODEL), lambda i: (i, 0)),
        out_shape=jax.ShapeDtypeStruct((n, D_MODEL), jnp.float32),
        compiler_params=pltpu.CompilerParams(dimension_semantics=("arbitrary",),
                                             vmem_limit_bytes=VMEM_LIMIT_BYTES),
        name=name,
    )(x, *params)


def kernel(x_prompt, x_sample, cache_k, cache_v, state_conv, rel_bias_table, attn_sink, g_pre_mix, w_in, w_dw, b_dw, ln_g, ln_b, w_conv_out, w_out, g_post_mix, g_pre_ffn, w_ffn_gate, w_ffn_up, w_ffn_down, g_post_ffn):
    depth = w_in.shape[0]
    assert depth == 1, "single-layer step"
    bf = jnp.bfloat16
    row = lambda a: a.reshape(1, -1)
    l = 0
    w_ua = w_in[l][:, _UA0:_UB0].reshape(D_MODEL, N_COL_BLOCKS, 1, 128)
    w_ub = w_in[l][:, _UB0:_GA0].reshape(D_MODEL, N_COL_BLOCKS, 1, 128)
    w_glu = jnp.concatenate([w_ua, w_ub], axis=2).reshape(D_MODEL, 2 * D_MODEL)
    w_in_k = jnp.concatenate([w_in[l][:, :_UA0], w_glu, w_in[l][:, _GA0:]], axis=1).astype(bf)
    mixer_params = (row(g_pre_mix[l]), w_in_k, w_dw[l], row(b_dw[l]), row(ln_g[l]), row(ln_b[l]),
                    w_conv_out[l].astype(bf), w_out[l].astype(bf), row(g_post_mix[l]))
    ffn_params = (row(g_pre_ffn[l]), w_ffn_gate[l].astype(bf), w_ffn_up[l].astype(bf),
                  w_ffn_down[l].astype(bf), row(g_post_ffn[l]))
    sink = attn_sink[l]

    B, S, _ = x_prompt.shape
    xp1, kp, vp, cp = _mixer(x_prompt, None, rel_bias_table, sink, mixer_params, nb=1, tb=512, ch=CHUNK)
    yp = _ffn(xp1.reshape(B * S, D_MODEL), ffn_params, tm=1024, name="ffn_prompt").reshape(B, S, D_MODEL)

    Bs, Ss, _ = x_sample.shape
    hist = (cache_k[l].reshape(Bs, WINDOW, KV_WIDTH), cache_v[l].reshape(Bs, WINDOW, KV_WIDTH), state_conv[l])
    xs1, ks, vs, cs = _mixer(x_sample, hist, rel_bias_table, sink, mixer_params, nb=8, tb=Ss, ch=Ss)
    ys = _ffn(xs1.reshape(Bs * Ss, D_MODEL), ffn_params, tm=1024, name="ffn_sample").reshape(Bs, Ss, D_MODEL)

    kv5 = lambda a: a.reshape(1, a.shape[0], WINDOW, N_KV_HEADS, HEAD_DIM)
    return (yp, ys, kv5(kp), kv5(vp), cp[None], kv5(ks), kv5(vs), cs[None])
```

```python
import functools
import math

import jax
import jax.numpy as jnp
from jax import lax
from jax.experimental import pallas as pl
from jax.experimental.pallas import tpu as pltpu

D_MODEL = 1024
HEAD_DIM = 64
N_HEADS = 16
N_KV_HEADS = 2
GROUP = N_HEADS // N_KV_HEADS
PAIRS = GROUP // 2
KV_WIDTH = N_KV_HEADS * HEAD_DIM
WINDOW = 128
CHUNK = 64
KEY_PAD = 256
CONV_WIDTH = 31
CONV_STATE = CONV_WIDTH - 1
CONV_HIST = 32
CONV_ROWS = 32
N_COL_BLOCKS = D_MODEL // 128
ROW_BLOCK = 16
FFN_HALVES = 2
D_FF = 2816
N_BUCKETS = 32
MAX_DISTANCE = 128
EPS = 1e-6
NEG_INF = -1e30
LOG2E = math.log2(math.e)
IN_WIDTH = D_MODEL + 2 * KV_WIDTH + 2 * D_MODEL + 2 * D_MODEL

_Q0 = 0
_K0 = D_MODEL
_V0 = _K0 + KV_WIDTH
_UA0 = _V0 + KV_WIDTH
_UB0 = _UA0 + D_MODEL
_GA0 = _UB0 + D_MODEL

VMEM_LIMIT_BYTES = 56 * 1024 * 1024


def _t5_bucket(rel):
    half = N_BUCKETS // 2
    max_exact = half // 2
    ret = jnp.where(rel > 0, half, 0)
    n = jnp.abs(rel)
    nf = jnp.maximum(n, 1).astype(jnp.float32)
    large = max_exact + (jnp.log(nf / max_exact) / math.log(MAX_DISTANCE / max_exact)
                         * (half - max_exact)).astype(jnp.int32)
    large = jnp.minimum(large, half - 1)
    return ret + jnp.where(n < max_exact, n, large)


def _rms(x, g):
    return x * lax.rsqrt(jnp.mean(x * x, axis=-1, keepdims=True) + EPS) * g


def _bf16_dot(a, w):
    return jnp.dot(a.astype(jnp.bfloat16), w, preferred_element_type=jnp.float32)


def _mixer_kernel(*refs, nb, tb, ch, carry_from_inputs):
    it = iter(refs)
    table_ref = next(it)
    sink_ref = next(it)
    bucket_ref = next(it)
    x_ref = next(it)
    if carry_from_inputs:
        ck_ref = next(it)
        cv_ref = next(it)
        cs_ref = next(it)
    g_pre_ref = next(it)
    w_in_ref = next(it)
    w_dw_ref = next(it)
    b_dw_ref = next(it)
    ln_g_ref = next(it)
    ln_b_ref = next(it)
    w_co_ref = next(it)
    w_out_ref = next(it)
    g_post_ref = next(it)
    x1_ref = next(it)
    kwin_ref = next(it)
    vwin_ref = next(it)
    cst_ref = next(it)
    bias_scr = next(it)
    kx_scr = next(it)
    vx_scr = next(it)
    u_scr = next(it)
    attn_scr = next(it)
    conv_scr = next(it)
    convo_scr = next(it)
    gate_scr = next(it)
    h_scr = next(it)
    q_scr = next(it)
    ya_scr = next(it)
    mix_scr = next(it)
    proj_scr = next(it)

    f32, bf16 = jnp.float32, jnp.bfloat16
    n_chunks = tb // ch
    key_rows = kx_scr.shape[3]
    first_step = (pl.program_id(0) == 0) & (pl.program_id(1) == 0)
    seq_tile = pl.program_id(1)
    low_lanes = lax.broadcasted_iota(jnp.int32, (1, KV_WIDTH), 1) < HEAD_DIM

    @pl.when(first_step)
    def _first_step():
        bucket = bucket_ref[...]
        pad_bias = jnp.where(bucket < 0, NEG_INF, 0.0)
        for h in range(N_HEADS):
            bias_scr[h] = pad_bias

        def body(b, carry):
            hit = bucket == b
            for h in range(N_HEADS):
                bias_scr[h] = jnp.where(hit, table_ref[b, h] * LOG2E, bias_scr[h])
            return carry

        lax.fori_loop(0, N_BUCKETS, body, 0)
        kx_scr[...] = jnp.zeros_like(kx_scr)
        vx_scr[:, :, :, :, 0:KV_WIDTH] = jnp.zeros(vx_scr.shape[:4] + (KV_WIDTH,), bf16)
        vx_scr[:, :, :, :, KV_WIDTH:2 * KV_WIDTH] = jnp.ones(vx_scr.shape[:4] + (KV_WIDTH,), bf16)

    def store_kv(b, row0, k, v):
        n = k.shape[0]
        v_sw = pltpu.roll(v, HEAD_DIM, axis=1)
        zero = jnp.zeros((), f32)
        k_l2 = k.astype(bf16).astype(f32) * LOG2E
        k_hi = k_l2.astype(bf16).astype(f32)
        for part, kk in enumerate((k_hi, k_l2 - k_hi)):
            kk_sw = pltpu.roll(kk, HEAD_DIM, axis=1)
            lanes = slice(part * KV_WIDTH, (part + 1) * KV_WIDTH)
            for kh, (same, swapped) in enumerate(((kk, kk_sw), (kk_sw, kk))):
                kx_scr[b, kh, 0, row0:row0 + n, lanes] = jnp.where(low_lanes, same, zero).astype(bf16)
                kx_scr[b, kh, 1, row0:row0 + n, lanes] = jnp.where(low_lanes, zero, swapped).astype(bf16)
        for kh, (same, swapped) in enumerate(((v, v_sw), (v_sw, v))):
            vx_scr[b, kh, 0, row0:row0 + n, 0:KV_WIDTH] = jnp.where(low_lanes, same, zero).astype(bf16)
            vx_scr[b, kh, 1, row0:row0 + n, 0:KV_WIDTH] = jnp.where(low_lanes, zero, swapped).astype(bf16)

    if carry_from_inputs:
        for b in range(nb):
            store_kv(b, 0, ck_ref[b], cv_ref[b])
            for j in range(N_COL_BLOCKS):
                u_scr[b, j, 0:8, :] = jnp.zeros((8, 128), f32)
                u_scr[b, j, CONV_HIST - CONV_STATE:CONV_HIST, :] = cs_ref[b, :, j * 128:(j + 1) * 128]
    else:
        @pl.when(seq_tile == 0)
        def _reset_history():
            for b in range(nb):
                for kh in range(N_KV_HEADS):
                    for par in range(2):
                        kx_scr[b, kh, par, 0:WINDOW, :] = jnp.zeros((WINDOW, 2 * KV_WIDTH), bf16)
                        vx_scr[b, kh, par, 0:WINDOW, 0:KV_WIDTH] = jnp.zeros((WINDOW, KV_WIDTH), bf16)
                u_scr[b, :, 0:CONV_HIST, :] = jnp.zeros((N_COL_BLOCKS, CONV_HIST, 128), f32)

        @pl.when(seq_tile != 0)
        def _roll_history():
            for b in range(nb):
                for kh in range(N_KV_HEADS):
                    for par in range(2):
                        kx_scr[b, kh, par, 0:WINDOW, :] = kx_scr[b, kh, par, tb:tb + WINDOW, :]
                        vx_scr[b, kh, par, 0:WINDOW, 0:KV_WIDTH] = vx_scr[b, kh, par, tb:tb + WINDOW, 0:KV_WIDTH]
                u_scr[b, :, 0:CONV_HIST, :] = u_scr[b, :, tb:tb + CONV_HIST, :]

    m_rows = nb * tb

    def row_blocks():
        for r0 in range(0, m_rows, ROW_BLOCK):
            b, t0 = divmod(r0, tb)
            yield r0, b, t0

    for r0, b, t0 in row_blocks():
        xb = x_ref[b, t0:t0 + ROW_BLOCK, :]
        h_scr[r0:r0 + ROW_BLOCK, :] = _rms(xb, g_pre_ref[...]).astype(bf16)

    def proj(c0, n):
        return jnp.dot(h_scr[...], w_in_ref[:, c0:c0 + n], preferred_element_type=f32)

    def do_q(i):
        c0 = i * 512
        q_scr[:, c0:c0 + 512] = (proj(_Q0 + c0, 512) * (HEAD_DIM ** -0.5)).astype(bf16)

    def do_kv():
        kv = proj(_K0, 2 * KV_WIDTH)
        k = kv[:, 0:KV_WIDTH]
        v = kv[:, KV_WIDTH:2 * KV_WIDTH]
        for b in range(nb):
            rows = slice(b * tb, (b + 1) * tb)
            store_kv(b, WINDOW, k[rows], v[rows])
            if tb >= WINDOW:
                kwin_ref[b] = k[b * tb + tb - WINDOW:(b + 1) * tb]
                vwin_ref[b] = v[b * tb + tb - WINDOW:(b + 1) * tb]
            else:
                kwin_ref[b, 0:WINDOW - tb, :] = ck_ref[b, tb:WINDOW, :]
                vwin_ref[b, 0:WINDOW - tb, :] = cv_ref[b, tb:WINDOW, :]
                kwin_ref[b, WINDOW - tb:WINDOW, :] = k[rows]
                vwin_ref[b, WINDOW - tb:WINDOW, :] = v[rows]

    def do_gate(i):
        c0 = i * 256
        gate_scr[:, c0:c0 + 256] = jax.nn.sigmoid(proj(_GA0 + c0, 256))

    side_tasks = [functools.partial(do_q, 0), functools.partial(do_q, 1), do_kv]
    side_tasks += [functools.partial(do_gate, i) for i in range(2 * D_MODEL // 256)]

    def glu_block(j):
        z = proj(_UA0 + j * 256, 256)
        return z[:, 0:128] * jax.nn.sigmoid(z[:, 128:256])

    units_per_block = nb * (tb // CONV_ROWS)
    total_units = N_COL_BLOCKS * units_per_block
    first_side = 2 * units_per_block
    side_after = [first_side + (k * (total_units - first_side)) // len(side_tasks) for k in range(len(side_tasks))]
    next_side = 0
    unit = 0
    u_next = glu_block(0)
    for j in range(N_COL_BLOCKS):
        u_cur = u_next
        cols = slice(j * 128, (j + 1) * 128)
        for b in range(nb):
            u_scr[b, j, CONV_HIST:CONV_HIST + tb, :] = u_cur[b * tb:(b + 1) * tb, :]
            cst_ref[b, :, cols] = u_scr[b, j, CONV_HIST + tb - CONV_STATE:CONV_HIST + tb, :]
        if j + 1 < N_COL_BLOCKS:
            u_next = glu_block(j + 1)
        for b in range(nb):
            for r in range(tb // CONV_ROWS):
                base = r * CONV_ROWS
                acc = jnp.broadcast_to(b_dw_ref[:, cols], (CONV_ROWS, 128))
                for t in range(CONV_WIDTH):
                    off = base + CONV_HIST - CONV_STATE + t
                    acc = acc + w_dw_ref[t:t + 1, cols] * u_scr[b, j, off:off + CONV_ROWS, :]
                conv_scr[b * tb + base:b * tb + base + CONV_ROWS, cols] = acc
                unit += 1
                while next_side < len(side_tasks) and side_after[next_side] <= unit:
                    side_tasks[next_side]()
                    next_side += 1
    while next_side < len(side_tasks):
        side_tasks[next_side]()
        next_side += 1

    key_iota = lax.broadcasted_iota(jnp.int32, (ch, KEY_PAD), 1)
    groups = [(b, c, kh) for b in range(nb) for c in range(n_chunks) for kh in range(N_KV_HEADS)]

    def scores(b, c, kh):
        r0 = b * tb + c * ch
        qp = jnp.concatenate(
            [q_scr[r0:r0 + ch, (kh * PAIRS + i) * 128:(kh * PAIRS + i + 1) * 128] for i in range(PAIRS)], axis=0)
        qq = jnp.concatenate([qp, qp], axis=1)
        out = []
        for par in range(2):
            kx = kx_scr[b, kh, par, c * ch:c * ch + KEY_PAD, :]
            out.append(lax.dot_general(qq, kx, (((1,), (1,)), ((), ())), preferred_element_type=f32))
        return out

    def softmax_pv(b, c, kh, s_pair):
        r0 = b * tb + c * ch
        need_mask = (not carry_from_inputs) and (c * ch < WINDOW)
        if need_mask:
            first_pos = seq_tile * tb + (c * ch - WINDOW)
            valid = (key_iota + first_pos) >= 0
        o_pair = []
        sink_terms = []
        for par in range(2):
            ps = []
            for i in range(PAIRS):
                hh = kh * GROUP + 2 * i + par
                sg = s_pair[par][i * ch:(i + 1) * ch] + bias_scr[hh]
                if need_mask:
                    sg = jnp.where(valid, sg, NEG_INF)
                sk = sink_ref[hh] * LOG2E
                m = jnp.maximum(jnp.max(sg, axis=-1, keepdims=True), sk)
                ps.append(jnp.exp2(sg - m).astype(bf16))
                sink_terms.append(jnp.exp2(sk - m))
            pcat = jnp.concatenate(ps, axis=0)
            vx = vx_scr[b, kh, par, c * ch:c * ch + KEY_PAD, :]
            o_pair.append(jnp.dot(pcat, vx, preferred_element_type=f32))
        for i in range(PAIRS):
            rows = slice(i * ch, (i + 1) * ch)
            blk = None
            for par in range(2):
                o = o_pair[par]
                den = o[rows, KV_WIDTH:2 * KV_WIDTH] + sink_terms[par * PAIRS + i]
                term = o[rows, 0:KV_WIDTH] * (1.0 / den)
                blk = term if blk is None else blk + term
            c0 = (kh * PAIRS + i) * 128
            attn_scr[r0:r0 + ch, c0:c0 + 128] = blk

    s_next = scores(*groups[0])
    for r0, b, t0 in row_blocks():
        y = conv_scr[r0:r0 + ROW_BLOCK, :]
        mu = jnp.mean(y, axis=-1, keepdims=True)
        yc = y - mu
        yn = yc * lax.rsqrt(jnp.mean(yc * yc, axis=-1, keepdims=True) + EPS) * ln_g_ref[...] + ln_b_ref[...]
        ya_scr[r0:r0 + ROW_BLOCK, :] = (yn * jax.nn.sigmoid(yn)).astype(bf16)

    n_co_chunks = D_MODEL // 256
    co_after = [((k + 1) * len(groups)) // (n_co_chunks + 1) for k in range(n_co_chunks)]
    next_co = 0
    for gi, grp in enumerate(groups):
        s_cur = s_next
        if gi + 1 < len(groups):
            s_next = scores(*groups[gi + 1])
        softmax_pv(*grp, s_cur)
        while next_co < n_co_chunks and co_after[next_co] <= gi + 1:
            c0 = next_co * 256
            convo_scr[:, c0:c0 + 256] = jnp.dot(ya_scr[...], w_co_ref[:, c0:c0 + 256], preferred_element_type=f32)
            next_co += 1

    for r0, b, t0 in row_blocks():
        rows = slice(r0, r0 + ROW_BLOCK)
        mix = gate_scr[rows, 0:D_MODEL] * attn_scr[rows, :] + gate_scr[rows, D_MODEL:2 * D_MODEL] * convo_scr[rows, :]
        mix_scr[rows, :] = mix.astype(bf16)
    for c0 in range(0, D_MODEL, 256):
        proj_scr[:, c0:c0 + 256] = jnp.dot(mix_scr[...], w_out_ref[:, c0:c0 + 256], preferred_element_type=f32)
    for r0, b, t0 in row_blocks():
        xb = x_ref[b, t0:t0 + ROW_BLOCK, :]
        x1_ref[b, t0:t0 + ROW_BLOCK, :] = xb + _rms(proj_scr[r0:r0 + ROW_BLOCK, :], g_post_ref[...])


def _const_spec(shape):
    zeros = (0,) * len(shape)
    return pl.BlockSpec(shape, lambda i, j: zeros, pipeline_mode=pl.Buffered(1))


def _mixer(x, hist, table, sink, params, *, nb, tb, ch):
    B, S, _ = x.shape
    carry_from_inputs = hist is not None
    nkeys = WINDOW + ch
    assert nkeys <= KEY_PAD
    rel = jnp.arange(nkeys)[None, :] - WINDOW - jnp.arange(ch)[:, None]
    bucket = jnp.pad(_t5_bucket(rel).astype(jnp.int32), ((0, 0), (0, KEY_PAD - nkeys)), constant_values=-1)
    g_pre, w_in, w_dw, b_dw, ln_g, ln_b, w_co, w_out, g_post = params
    key_rows = (tb // ch - 1) * ch + KEY_PAD

    grid = (B // nb, S // tb)
    smem = pl.BlockSpec(memory_space=pltpu.SMEM)
    in_specs = [smem, smem, _const_spec((ch, KEY_PAD)),
                pl.BlockSpec((nb, tb, D_MODEL), lambda i, j: (i, j, 0))]
    args = [table, sink, bucket, x]
    if carry_from_inputs:
        in_specs += [pl.BlockSpec((nb, WINDOW, KV_WIDTH), lambda i, j: (i, 0, 0)),
                     pl.BlockSpec((nb, WINDOW, KV_WIDTH), lambda i, j: (i, 0, 0)),
                     pl.BlockSpec((nb, CONV_STATE, D_MODEL), lambda i, j: (i, 0, 0))]
        args += list(hist)
    weights = [g_pre, w_in, w_dw, b_dw, ln_g, ln_b, w_co, w_out, g_post]
    in_specs += [_const_spec(w.shape) for w in weights]
    args += weights

    out_shape = (jax.ShapeDtypeStruct((B, S, D_MODEL), jnp.float32),
                 jax.ShapeDtypeStruct((B, WINDOW, KV_WIDTH), jnp.float32),
                 jax.ShapeDtypeStruct((B, WINDOW, KV_WIDTH), jnp.float32),
                 jax.ShapeDtypeStruct((B, CONV_STATE, D_MODEL), jnp.float32))
    out_specs = (pl.BlockSpec((nb, tb, D_MODEL), lambda i, j: (i, j, 0)),
                 pl.BlockSpec((nb, WINDOW, KV_WIDTH), lambda i, j: (i, 0, 0)),
                 pl.BlockSpec((nb, WINDOW, KV_WIDTH), lambda i, j: (i, 0, 0)),
                 pl.BlockSpec((nb, CONV_STATE, D_MODEL), lambda i, j: (i, 0, 0)))
    f32, bf16 = jnp.float32, jnp.bfloat16
    scratch = [pltpu.VMEM((N_HEADS, ch, KEY_PAD), f32),
               pltpu.VMEM((nb, N_KV_HEADS, 2, key_rows, 2 * KV_WIDTH), bf16),
               pltpu.VMEM((nb, N_KV_HEADS, 2, key_rows, 2 * KV_WIDTH), bf16),
               pltpu.VMEM((nb, N_COL_BLOCKS, CONV_HIST + tb, 128), f32),
               pltpu.VMEM((nb * tb, D_MODEL), f32),
               pltpu.VMEM((nb * tb, D_MODEL), f32),
               pltpu.VMEM((nb * tb, D_MODEL), f32),
               pltpu.VMEM((nb * tb, 2 * D_MODEL), f32),
               pltpu.VMEM((nb * tb, D_MODEL), bf16),
               pltpu.VMEM((nb * tb, D_MODEL), bf16),
               pltpu.VMEM((nb * tb, D_MODEL), bf16),
               pltpu.VMEM((nb * tb, D_MODEL), bf16),
               pltpu.VMEM((nb * tb, D_MODEL), f32)]
    kern = functools.partial(_mixer_kernel, nb=nb, tb=tb, ch=ch, carry_from_inputs=carry_from_inputs)
    return pl.pallas_call(
        kern, grid=grid, in_specs=in_specs, out_specs=out_specs, out_shape=out_shape,
        scratch_shapes=scratch,
        compiler_params=pltpu.CompilerParams(dimension_semantics=("arbitrary", "arbitrary"),
                                             vmem_limit_bytes=VMEM_LIMIT_BYTES),
        name="mixer_sample" if carry_from_inputs else "mixer_prompt",
    )(*args)


def _ffn_kernel(x_ref, g_pre_ref, wg_ref, wu_ref, wd_ref, g_post_ref, y_ref):
    tm = x_ref.shape[0]
    hm = tm // FFN_HALVES
    hs = [_rms(x_ref[i * hm:(i + 1) * hm, :], g_pre_ref[...]).astype(jnp.bfloat16) for i in range(FFN_HALVES)]
    for i in range(FFN_HALVES):
        rows = slice(i * hm, (i + 1) * hm)
        a = jnp.dot(hs[i], wg_ref[...], preferred_element_type=jnp.float32)
        b = jnp.dot(hs[i], wu_ref[...], preferred_element_type=jnp.float32)
        act = a * jax.nn.sigmoid(a) * b
        f = _bf16_dot(act, wd_ref[...])
        y_ref[rows, :] = x_ref[rows, :] + _rms(f, g_post_ref[...])


def _ffn(x, params, *, tm, name):
    n = x.shape[0]
    const = lambda shape: pl.BlockSpec(shape, lambda i: (0,) * len(shape), pipeline_mode=pl.Buffered(1))
    return pl.pallas_call(
        _ffn_kernel, grid=(n // tm,),
        in_specs=[pl.BlockSpec((tm, D_MODEL), lambda i: (i, 0))] + [const(w.shape) for w in params],
        out_specs=pl.BlockSpec((tm, D_MODEL), lambda i: (i, 0)),
        out_shape=jax.ShapeDtypeStruct((n, D_MODEL), jnp.float32),
        compiler_params=pltpu.CompilerParams(dimension_semantics=("arbitrary",),
                                             vmem_limit_bytes=VMEM_LIMIT_BYTES),
        name=name,
    )(x, *params)


def kernel(x_prompt, x_sample, cache_k, cache_v, state_conv, rel_bias_table, attn_sink, g_pre_mix, w_in, w_dw, b_dw, ln_g, ln_b, w_conv_out, w_out, g_post_mix, g_pre_ffn, w_ffn_gate, w_ffn_up, w_ffn_down, g_post_ffn):
    depth = w_in.shape[0]
    assert depth == 1, "single-layer step"
    bf = jnp.bfloat16
    row = lambda a: a.reshape(1, -1)
    l = 0
    w_ua = w_in[l][:, _UA0:_UB0].reshape(D_MODEL, N_COL_BLOCKS, 1, 128)
    w_ub = w_in[l][:, _UB0:_GA0].reshape(D_MODEL, N_COL_BLOCKS, 1, 128)
    w_glu = jnp.concatenate([w_ua, w_ub], axis=2).reshape(D_MODEL, 2 * D_MODEL)
    w_in_k = jnp.concatenate([w_in[l][:, :_UA0], w_glu, w_in[l][:, _GA0:]], axis=1).astype(bf)
    mixer_params = (row(g_pre_mix[l]), w_in_k, w_dw[l], row(b_dw[l]), row(ln_g[l]), row(ln_b[l]),
                    w_conv_out[l].astype(bf), w_out[l].astype(bf), row(g_post_mix[l]))
    ffn_params = (row(g_pre_ffn[l]), w_ffn_gate[l].astype(bf), w_ffn_up[l].astype(bf),
                  w_ffn_down[l].astype(bf), row(g_post_ffn[l]))
    sink = attn_sink[l]

    B, S, _ = x_prompt.shape
    xp1, kp, vp, cp = _mixer(x_prompt, None, rel_bias_table, sink, mixer_params, nb=1, tb=512, ch=CHUNK)
    yp = _ffn(xp1.reshape(B * S, D_MODEL), ffn_params, tm=1024, name="ffn_prompt").reshape(B, S, D_MODEL)

    Bs, Ss, _ = x_sample.shape
    hist = (cache_k[l].reshape(Bs, WINDOW, KV_WIDTH), cache_v[l].reshape(Bs, WINDOW, KV_WIDTH), state_conv[l])
    xs1, ks, vs, cs = _mixer(x_sample, hist, rel_bias_table, sink, mixer_params, nb=8, tb=Ss, ch=Ss)
    ys = _ffn(xs1.reshape(Bs * Ss, D_MODEL), ffn_params, tm=1024, name="ffn_sample").reshape(Bs, Ss, D_MODEL)

    kv5 = lambda a: a.reshape(1, a.shape[0], WINDOW, N_KV_HEADS, HEAD_DIM)
    return (yp, ys, kv5(kp), kv5(vp), cp[None], kv5(ks), kv5(vs), cs[None])
```

```python
import functools
import math

import jax
import jax.numpy as jnp
from jax import lax
from jax.experimental import pallas as pl
from jax.experimental.pallas import tpu as pltpu

D_MODEL = 1024
HEAD_DIM = 64
N_HEADS = 16
N_KV_HEADS = 2
GROUP = N_HEADS // N_KV_HEADS
PAIRS = GROUP // 2
KV_WIDTH = N_KV_HEADS * HEAD_DIM
WINDOW = 128
CHUNK = 64
KEY_PAD = 256
CONV_WIDTH = 31
CONV_STATE = CONV_WIDTH - 1
CONV_HIST = 32
CONV_ROWS = 32
N_COL_BLOCKS = D_MODEL // 128
ROW_BLOCK = 16
FFN_HALVES = 2
D_FF = 2816
N_BUCKETS = 32
MAX_DISTANCE = 128
EPS = 1e-6
NEG_INF = -1e30
LOG2E = math.log2(math.e)
IN_WIDTH = D_MODEL + 2 * KV_WIDTH + 2 * D_MODEL + 2 * D_MODEL

_Q0 = 0
_K0 = D_MODEL
_V0 = _K0 + KV_WIDTH
_UA0 = _V0 + KV_WIDTH
_UB0 = _UA0 + D_MODEL
_GA0 = _UB0 + D_MODEL

VMEM_LIMIT_BYTES = 56 * 1024 * 1024


def _t5_bucket(rel):
    half = N_BUCKETS // 2
    max_exact = half // 2
    ret = jnp.where(rel > 0, half, 0)
    n = jnp.abs(rel)
    nf = jnp.maximum(n, 1).astype(jnp.float32)
    large = max_exact + (jnp.log(nf / max_exact) / math.log(MAX_DISTANCE / max_exact)
                         * (half - max_exact)).astype(jnp.int32)
    large = jnp.minimum(large, half - 1)
    return ret + jnp.where(n < max_exact, n, large)


def _rms(x, g):
    return x * lax.rsqrt(jnp.mean(x * x, axis=-1, keepdims=True) + EPS) * g


def _bf16_dot(a, w):
    return jnp.dot(a.astype(jnp.bfloat16), w, preferred_element_type=jnp.float32)


def _mixer_kernel(*refs, nb, tb, ch, carry_from_inputs):
    it = iter(refs)
    table_ref = next(it)
    sink_ref = next(it)
    bucket_ref = next(it)
    x_ref = next(it)
    if carry_from_inputs:
        ck_ref = next(it)
        cv_ref = next(it)
        cs_ref = next(it)
    g_pre_ref = next(it)
    w_in_ref = next(it)
    w_dw_ref = next(it)
    b_dw_ref = next(it)
    ln_g_ref = next(it)
    ln_b_ref = next(it)
    w_co_ref = next(it)
    w_out_ref = next(it)
    g_post_ref = next(it)
    x1_ref = next(it)
    kwin_ref = next(it)
    vwin_ref = next(it)
    cst_ref = next(it)
    bias_scr = next(it)
    kx_scr = next(it)
    vx_scr = next(it)
    u_scr = next(it)
    attn_scr = next(it)
    conv_scr = next(it)
    convo_scr = next(it)
    gate_scr = next(it)
    h_scr = next(it)
    q_scr = next(it)
    ya_scr = next(it)
    mix_scr = next(it)
    proj_scr = next(it)

    f32, bf16 = jnp.float32, jnp.bfloat16
    n_chunks = tb // ch
    key_rows = kx_scr.shape[3]
    first_step = (pl.program_id(0) == 0) & (pl.program_id(1) == 0)
    seq_tile = pl.program_id(1)
    low_lanes = lax.broadcasted_iota(jnp.int32, (1, KV_WIDTH), 1) < HEAD_DIM

    @pl.when(first_step)
    def _first_step():
        bucket = bucket_ref[...]
        pad_bias = jnp.where(bucket < 0, NEG_INF, 0.0)
        for h in range(N_HEADS):
            bias_scr[h] = pad_bias

        def body(b, carry):
            hit = bucket == b
            for h in range(N_HEADS):
                bias_scr[h] = jnp.where(hit, table_ref[b, h] * LOG2E, bias_scr[h])
            return carry

        lax.fori_loop(0, N_BUCKETS, body, 0)
        kx_scr[...] = jnp.zeros_like(kx_scr)
        vx_scr[:, :, :, :, 0:KV_WIDTH] = jnp.zeros(vx_scr.shape[:4] + (KV_WIDTH,), bf16)
        vx_scr[:, :, :, :, KV_WIDTH:2 * KV_WIDTH] = jnp.ones(vx_scr.shape[:4] + (KV_WIDTH,), bf16)

    def store_kv(b, row0, k, v):
        n = k.shape[0]
        v_sw = pltpu.roll(v, HEAD_DIM, axis=1)
        zero = jnp.zeros((), f32)
        k_l2 = k.astype(bf16).astype(f32) * LOG2E
        k_hi = k_l2.astype(bf16).astype(f32)
        for part, kk in enumerate((k_hi, k_l2 - k_hi)):
            kk_sw = pltpu.roll(kk, HEAD_DIM, axis=1)
            lanes = slice(part * KV_WIDTH, (part + 1) * KV_WIDTH)
            for kh, (same, swapped) in enumerate(((kk, kk_sw), (kk_sw, kk))):
                kx_scr[b, kh, 0, row0:row0 + n, lanes] = jnp.where(low_lanes, same, zero).astype(bf16)
                kx_scr[b, kh, 1, row0:row0 + n, lanes] = jnp.where(low_lanes, zero, swapped).astype(bf16)
        for kh, (same, swapped) in enumerate(((v, v_sw), (v_sw, v))):
            vx_scr[b, kh, 0, row0:row0 + n, 0:KV_WIDTH] = jnp.where(low_lanes, same, zero).astype(bf16)
            vx_scr[b, kh, 1, row0:row0 + n, 0:KV_WIDTH] = jnp.where(low_lanes, zero, swapped).astype(bf16)

    if carry_from_inputs:
        for b in range(nb):
            store_kv(b, 0, ck_ref[b], cv_ref[b])
            for j in range(N_COL_BLOCKS):
                u_scr[b, j, 0:8, :] = jnp.zeros((8, 128), f32)
                u_scr[b, j, CONV_HIST - CONV_STATE:CONV_HIST, :] = cs_ref[b, :, j * 128:(j + 1) * 128]
    else:
        @pl.when(seq_tile == 0)
        def _reset_history():
            for b in range(nb):
                for kh in range(N_KV_HEADS):
                    for par in range(2):
                        kx_scr[b, kh, par, 0:WINDOW, :] = jnp.zeros((WINDOW, 2 * KV_WIDTH), bf16)
                        vx_scr[b, kh, par, 0:WINDOW, 0:KV_WIDTH] = jnp.zeros((WINDOW, KV_WIDTH), bf16)
                u_scr[b, :, 0:CONV_HIST, :] = jnp.zeros((N_COL_BLOCKS, CONV_HIST, 128), f32)

        @pl.when(seq_tile != 0)
        def _roll_history():
            for b in range(nb):
                for kh in range(N_KV_HEADS):
                    for par in range(2):
                        kx_scr[b, kh, par, 0:WINDOW, :] = kx_scr[b, kh, par, tb:tb + WINDOW, :]
                        vx_scr[b, kh, par, 0:WINDOW, 0:KV_WIDTH] = vx_scr[b, kh, par, tb:tb + WINDOW, 0:KV_WIDTH]
                u_scr[b, :, 0:CONV_HIST, :] = u_scr[b, :, tb:tb + CONV_HIST, :]

    m_rows = nb * tb

    def row_blocks():
        for r0 in range(0, m_rows, ROW_BLOCK):
            b, t0 = divmod(r0, tb)
            yield r0, b, t0

    for r0, b, t0 in row_blocks():
        xb = x_ref[b, t0:t0 + ROW_BLOCK, :]
        h_scr[r0:r0 + ROW_BLOCK, :] = _rms(xb, g_pre_ref[...]).astype(bf16)

    def proj(c0, n):
        acc = None
        for k0 in range(0, D_MODEL, 256):
            part = jnp.dot(h_scr[:, k0:k0 + 256], w_in_ref[k0:k0 + 256, c0:c0 + n], preferred_element_type=f32)
            acc = part if acc is None else acc + part
        return acc

    def do_q(i):
        c0 = i * 512
        q_scr[:, c0:c0 + 512] = (proj(_Q0 + c0, 512) * (HEAD_DIM ** -0.5)).astype(bf16)

    def do_kv():
        kv = proj(_K0, 2 * KV_WIDTH)
        k = kv[:, 0:KV_WIDTH]
        v = kv[:, KV_WIDTH:2 * KV_WIDTH]
        for b in range(nb):
            rows = slice(b * tb, (b + 1) * tb)
            store_kv(b, WINDOW, k[rows], v[rows])
            if tb >= WINDOW:
                kwin_ref[b] = k[b * tb + tb - WINDOW:(b + 1) * tb]
                vwin_ref[b] = v[b * tb + tb - WINDOW:(b + 1) * tb]
            else:
                kwin_ref[b, 0:WINDOW - tb, :] = ck_ref[b, tb:WINDOW, :]
                vwin_ref[b, 0:WINDOW - tb, :] = cv_ref[b, tb:WINDOW, :]
                kwin_ref[b, WINDOW - tb:WINDOW, :] = k[rows]
                vwin_ref[b, WINDOW - tb:WINDOW, :] = v[rows]

    def do_gate(i):
        c0 = i * 256
        gate_scr[:, c0:c0 + 256] = jax.nn.sigmoid(proj(_GA0 + c0, 256))

    side_tasks = [functools.partial(do_q, 0), functools.partial(do_q, 1), do_kv]
    side_tasks += [functools.partial(do_gate, i) for i in range(2 * D_MODEL // 256)]

    def glu_block(j):
        z = proj(_UA0 + j * 256, 256)
        return z[:, 0:128] * jax.nn.sigmoid(z[:, 128:256])

    units_per_block = nb * (tb // CONV_ROWS)
    total_units = N_COL_BLOCKS * units_per_block
    first_side = 2 * units_per_block
    side_after = [first_side + (k * (total_units - first_side)) // len(side_tasks) for k in range(len(side_tasks))]
    next_side = 0
    unit = 0
    u_next = glu_block(0)
    for j in range(N_COL_BLOCKS):
        u_cur = u_next
        cols = slice(j * 128, (j + 1) * 128)
        for b in range(nb):
            u_scr[b, j, CONV_HIST:CONV_HIST + tb, :] = u_cur[b * tb:(b + 1) * tb, :]
            cst_ref[b, :, cols] = u_scr[b, j, CONV_HIST + tb - CONV_STATE:CONV_HIST + tb, :]
        if j + 1 < N_COL_BLOCKS:
            u_next = glu_block(j + 1)
        for b in range(nb):
            for r in range(tb // CONV_ROWS):
                base = r * CONV_ROWS
                acc = jnp.broadcast_to(b_dw_ref[:, cols], (CONV_ROWS, 128))
                for t in range(CONV_WIDTH):
                    off = base + CONV_HIST - CONV_STATE + t
                    acc = acc + w_dw_ref[t:t + 1, cols] * u_scr[b, j, off:off + CONV_ROWS, :]
                conv_scr[b * tb + base:b * tb + base + CONV_ROWS, cols] = acc
                unit += 1
                while next_side < len(side_tasks) and side_after[next_side] <= unit:
                    side_tasks[next_side]()
                    next_side += 1
    while next_side < len(side_tasks):
        side_tasks[next_side]()
        next_side += 1

    key_iota = lax.broadcasted_iota(jnp.int32, (ch, KEY_PAD), 1)
    groups = [(b, c, kh) for b in range(nb) for c in range(n_chunks) for kh in range(N_KV_HEADS)]

    def scores(b, c, kh):
        r0 = b * tb + c * ch
        qp = jnp.concatenate(
            [q_scr[r0:r0 + ch, (kh * PAIRS + i) * 128:(kh * PAIRS + i + 1) * 128] for i in range(PAIRS)], axis=0)
        qq = jnp.concatenate([qp, qp], axis=1)
        out = []
        for par in range(2):
            kx = kx_scr[b, kh, par, c * ch:c * ch + KEY_PAD, :]
            out.append(lax.dot_general(qq, kx, (((1,), (1,)), ((), ())), preferred_element_type=f32))
        return out

    def softmax_pv(b, c, kh, s_pair):
        r0 = b * tb + c * ch
        need_mask = (not carry_from_inputs) and (c * ch < WINDOW)
        if need_mask:
            first_pos = seq_tile * tb + (c * ch - WINDOW)
            valid = (key_iota + first_pos) >= 0
        o_pair = []
        sink_terms = []
        for par in range(2):
            ps = []
            for i in range(PAIRS):
                hh = kh * GROUP + 2 * i + par
                sg = s_pair[par][i * ch:(i + 1) * ch] + bias_scr[hh]
                if need_mask:
                    sg = jnp.where(valid, sg, NEG_INF)
                sk = sink_ref[hh] * LOG2E
                m = jnp.maximum(jnp.max(sg, axis=-1, keepdims=True), sk)
                ps.append(jnp.exp2(sg - m).astype(bf16))
                sink_terms.append(jnp.exp2(sk - m))
            pcat = jnp.concatenate(ps, axis=0)
            vx = vx_scr[b, kh, par, c * ch:c * ch + KEY_PAD, :]
            o_pair.append(jnp.dot(pcat, vx, preferred_element_type=f32))
        for i in range(PAIRS):
            rows = slice(i * ch, (i + 1) * ch)
            blk = None
            for par in range(2):
                o = o_pair[par]
                den = o[rows, KV_WIDTH:2 * KV_WIDTH] + sink_terms[par * PAIRS + i]
                term = o[rows, 0:KV_WIDTH] * (1.0 / den)
                blk = term if blk is None else blk + term
            c0 = (kh * PAIRS + i) * 128
            attn_scr[r0:r0 + ch, c0:c0 + 128] = blk

    s_next = scores(*groups[0])
    for r0, b, t0 in row_blocks():
        y = conv_scr[r0:r0 + ROW_BLOCK, :]
        mu = jnp.mean(y, axis=-1, keepdims=True)
        yc = y - mu
        yn = yc * lax.rsqrt(jnp.mean(yc * yc, axis=-1, keepdims=True) + EPS) * ln_g_ref[...] + ln_b_ref[...]
        ya_scr[r0:r0 + ROW_BLOCK, :] = (yn * jax.nn.sigmoid(yn)).astype(bf16)

    n_co_chunks = D_MODEL // 256
    co_after = [((k + 1) * len(groups)) // (n_co_chunks + 1) for k in range(n_co_chunks)]
    next_co = 0
    for gi, grp in enumerate(groups):
        s_cur = s_next
        if gi + 1 < len(groups):
            s_next = scores(*groups[gi + 1])
        softmax_pv(*grp, s_cur)
        while next_co < n_co_chunks and co_after[next_co] <= gi + 1:
            c0 = next_co * 256
            convo_scr[:, c0:c0 + 256] = jnp.dot(ya_scr[...], w_co_ref[:, c0:c0 + 256], preferred_element_type=f32)
            next_co += 1

    for r0, b, t0 in row_blocks():
        rows = slice(r0, r0 + ROW_BLOCK)
        mix = gate_scr[rows, 0:D_MODEL] * attn_scr[rows, :] + gate_scr[rows, D_MODEL:2 * D_MODEL] * convo_scr[rows, :]
        mix_scr[rows, :] = mix.astype(bf16)
    for c0 in range(0, D_MODEL, 256):
        proj_scr[:, c0:c0 + 256] = jnp.dot(mix_scr[...], w_out_ref[:, c0:c0 + 256], preferred_element_type=f32)
    for r0, b, t0 in row_blocks():
        xb = x_ref[b, t0:t0 + ROW_BLOCK, :]
        x1_ref[b, t0:t0 + ROW_BLOCK, :] = xb + _rms(proj_scr[r0:r0 + ROW_BLOCK, :], g_post_ref[...])


def _const_spec(shape):
    zeros = (0,) * len(shape)
    return pl.BlockSpec(shape, lambda i, j: zeros, pipeline_mode=pl.Buffered(1))


def _mixer(x, hist, table, sink, params, *, nb, tb, ch):
    B, S, _ = x.shape
    carry_from_inputs = hist is not None
    nkeys = WINDOW + ch
    assert nkeys <= KEY_PAD
    rel = jnp.arange(nkeys)[None, :] - WINDOW - jnp.arange(ch)[:, None]
    bucket = jnp.pad(_t5_bucket(rel).astype(jnp.int32), ((0, 0), (0, KEY_PAD - nkeys)), constant_values=-1)
    g_pre, w_in, w_dw, b_dw, ln_g, ln_b, w_co, w_out, g_post = params
    key_rows = (tb // ch - 1) * ch + KEY_PAD

    grid = (B // nb, S // tb)
    smem = pl.BlockSpec(memory_space=pltpu.SMEM)
    in_specs = [smem, smem, _const_spec((ch, KEY_PAD)),
                pl.BlockSpec((nb, tb, D_MODEL), lambda i, j: (i, j, 0))]
    args = [table, sink, bucket, x]
    if carry_from_inputs:
        in_specs += [pl.BlockSpec((nb, WINDOW, KV_WIDTH), lambda i, j: (i, 0, 0)),
                     pl.BlockSpec((nb, WINDOW, KV_WIDTH), lambda i, j: (i, 0, 0)),
                     pl.BlockSpec((nb, CONV_STATE, D_MODEL), lambda i, j: (i, 0, 0))]
        args += list(hist)
    weights = [g_pre, w_in, w_dw, b_dw, ln_g, ln_b, w_co, w_out, g_post]
    in_specs += [_const_spec(w.shape) for w in weights]
    args += weights

    out_shape = (jax.ShapeDtypeStruct((B, S, D_MODEL), jnp.float32),
                 jax.ShapeDtypeStruct((B, WINDOW, KV_WIDTH), jnp.float32),
                 jax.ShapeDtypeStruct((B, WINDOW, KV_WIDTH), jnp.float32),
                 jax.ShapeDtypeStruct((B, CONV_STATE, D_MODEL), jnp.float32))
    out_specs = (pl.BlockSpec((nb, tb, D_MODEL), lambda i, j: (i, j, 0)),
                 pl.BlockSpec((nb, WINDOW, KV_WIDTH), lambda i, j: (i, 0, 0)),
                 pl.BlockSpec((nb, WINDOW, KV_WIDTH), lambda i, j: (i, 0, 0)),
                 pl.BlockSpec((nb, CONV_STATE, D_MODEL), lambda i, j: (i, 0, 0)))
    f32, bf16 = jnp.float32, jnp.bfloat16
    scratch = [pltpu.VMEM((N_HEADS, ch, KEY_PAD), f32),
               pltpu.VMEM((nb, N_KV_HEADS, 2, key_rows, 2 * KV_WIDTH), bf16),
               pltpu.VMEM((nb, N_KV_HEADS, 2, key_rows, 2 * KV_WIDTH), bf16),
               pltpu.VMEM((nb, N_COL_BLOCKS, CONV_HIST + tb, 128), f32),
               pltpu.VMEM((nb * tb, D_MODEL), f32),
               pltpu.VMEM((nb * tb, D_MODEL), f32),
               pltpu.VMEM((nb * tb, D_MODEL), f32),
               pltpu.VMEM((nb * tb, 2 * D_MODEL), f32),
               pltpu.VMEM((nb * tb, D_MODEL), bf16),
               pltpu.VMEM((nb * tb, D_MODEL), bf16),
               pltpu.VMEM((nb * tb, D_MODEL), bf16),
               pltpu.VMEM((nb * tb, D_MODEL), bf16),
               pltpu.VMEM((nb * tb, D_MODEL), f32)]
    kern = functools.partial(_mixer_kernel, nb=nb, tb=tb, ch=ch, carry_from_inputs=carry_from_inputs)
    return pl.pallas_call(
        kern, grid=grid, in_specs=in_specs, out_specs=out_specs, out_shape=out_shape,
        scratch_shapes=scratch,
        compiler_params=pltpu.CompilerParams(dimension_semantics=("arbitrary", "arbitrary"),
                                             vmem_limit_bytes=VMEM_LIMIT_BYTES),
        name="mixer_sample" if carry_from_inputs else "mixer_prompt",
    )(*args)


def _ffn_kernel(x_ref, g_pre_ref, wg_ref, wu_ref, wd_ref, g_post_ref, y_ref):
    tm = x_ref.shape[0]
    hm = tm // FFN_HALVES
    hs = [_rms(x_ref[i * hm:(i + 1) * hm, :], g_pre_ref[...]).astype(jnp.bfloat16) for i in range(FFN_HALVES)]
    for i in range(FFN_HALVES):
        rows = slice(i * hm, (i + 1) * hm)
        a = jnp.dot(hs[i], wg_ref[...], preferred_element_type=jnp.float32)
        b = jnp.dot(hs[i], wu_ref[...], preferred_element_type=jnp.float32)
        act = a * jax.nn.sigmoid(a) * b
        f = _bf16_dot(act, wd_ref[...])
        y_ref[rows, :] = x_ref[rows, :] + _rms(f, g_post_ref[...])


def _ffn(x, params, *, tm, name):
    n = x.shape[0]
    const = lambda shape: pl.BlockSpec(shape, lambda i: (0,) * len(shape), pipeline_mode=pl.Buffered(1))
    return pl.pallas_call(
        _ffn_kernel, grid=(n // tm,),
        in_specs=[pl.BlockSpec((tm, D_MODEL), lambda i: (i, 0))] + [const(w.shape) for w in params],
        out_specs=pl.BlockSpec((tm, D_MODEL), lambda i: (i, 0)),
        out_shape=jax.ShapeDtypeStruct((n, D_MODEL), jnp.float32),
        compiler_params=pltpu.CompilerParams(dimension_semantics=("arbitrary",),
                                             vmem_limit_bytes=VMEM_LIMIT_BYTES),
        name=name,
    )(x, *params)


def kernel(x_prompt, x_sample, cache_k, cache_v, state_conv, rel_bias_table, attn_sink, g_pre_mix, w_in, w_dw, b_dw, ln_g, ln_b, w_conv_out, w_out, g_post_mix, g_pre_ffn, w_ffn_gate, w_ffn_up, w_ffn_down, g_post_ffn):
    depth = w_in.shape[0]
    assert depth == 1, "single-layer step"
    bf = jnp.bfloat16
    row = lambda a: a.reshape(1, -1)
    l = 0
    w_ua = w_in[l][:, _UA0:_UB0].reshape(D_MODEL, N_COL_BLOCKS, 1, 128)
    w_ub = w_in[l][:, _UB0:_GA0].reshape(D_MODEL, N_COL_BLOCKS, 1, 128)
    w_glu = jnp.concatenate([w_ua, w_ub], axis=2).reshape(D_MODEL, 2 * D_MODEL)
    w_in_k = jnp.concatenate([w_in[l][:, :_UA0], w_glu, w_in[l][:, _GA0:]], axis=1).astype(bf)
    mixer_params = (row(g_pre_mix[l]), w_in_k, w_dw[l], row(b_dw[l]), row(ln_g[l]), row(ln_b[l]),
                    w_conv_out[l].astype(bf), w_out[l].astype(bf), row(g_post_mix[l]))
    ffn_params = (row(g_pre_ffn[l]), w_ffn_gate[l].astype(bf), w_ffn_up[l].astype(bf),
                  w_ffn_down[l].astype(bf), row(g_post_ffn[l]))
    sink = attn_sink[l]

    B, S, _ = x_prompt.shape
    xp1, kp, vp, cp = _mixer(x_prompt, None, rel_bias_table, sink, mixer_params, nb=1, tb=512, ch=CHUNK)
    yp = _ffn(xp1.reshape(B * S, D_MODEL), ffn_params, tm=1024, name="ffn_prompt").reshape(B, S, D_MODEL)

    Bs, Ss, _ = x_sample.shape
    hist = (cache_k[l].reshape(Bs, WINDOW, KV_WIDTH), cache_v[l].reshape(Bs, WINDOW, KV_WIDTH), state_conv[l])
    xs1, ks, vs, cs = _mixer(x_sample, hist, rel_bias_table, sink, mixer_params, nb=8, tb=Ss, ch=Ss)
    ys = _ffn(xs1.reshape(Bs * Ss, D_MODEL), ffn_params, tm=1024, name="ffn_sample").reshape(Bs, Ss, D_MODEL)

    kv5 = lambda a: a.reshape(1, a.shape[0], WINDOW, N_KV_HEADS, HEAD_DIM)
    return (yp, ys, kv5(kp), kv5(vp), cp[None], kv5(ks), kv5(vs), cs[None])
```
